```python
import jax, jax.numpy as jnp
from jax import lax
import numpy as np

D_MODEL = 2048
BATCH = 2
SEQ = 8192
DEPTH = 2

N_EVEN = (DEPTH + 1) // 2
N_ODD = DEPTH // 2
MIX_WIDTH = D_MODEL

A_HEAD_DIM = 128
A_WIDTH = D_MODEL // 2
A_HEADS = A_WIDTH // A_HEAD_DIM
MOBA_BLOCK = 256
MOBA_TOPK = 3
MOBA_Q_CHUNK = 32
B_WIDTH = MIX_WIDTH - A_WIDTH
B_CONV = 3
AB_IN = 3 * A_WIDTH + 3 * B_WIDTH
C_WIDTH = D_MODEL // 2
C_BLOCKS = 8
C_BLOCK_DIM = C_WIDTH // C_BLOCKS
C_CONV = 4
RG_C = 8.0
D_WIDTH = MIX_WIDTH - C_WIDTH
D_HEADS = 4
D_HEAD_DIM = D_WIDTH // D_HEADS
MLSTM_CHUNK = 64
CD_IN = 2 * C_WIDTH + 4 * D_WIDTH + 2 * D_HEADS
FFN_HIDDEN = -(-8 * D_MODEL // (3 * 256)) * 256

EPS = 1e-6
NEG_INF = -1e30

kernel_name = "hybrid_moba_conv_rglru_mlstm_adaln"


def rmsnorm(x, g):
    xf = x.astype(jnp.float32)
    xf = xf * lax.rsqrt(jnp.mean(xf * xf, axis=-1, keepdims=True) + EPS)
    return xf.astype(x.dtype) * g


def alibi_slopes(n_heads):
    return 2.0 ** (-8.0 * jnp.arange(1, n_heads + 1, dtype=jnp.float32) / n_heads)


def causal_depthwise_conv(u, w):
    k_w, ch = w.shape
    return lax.conv_general_dilated(
        u, w[:, None, :].astype(u.dtype), window_strides=(1,), padding=[(k_w - 1, 0)],
        dimension_numbers=('NWC', 'WIO', 'NWC'), feature_group_count=ch)


def moba_attention(q, k, v, slopes):
    bsz, nh, s, dh = q.shape
    nb = -(-s // MOBA_BLOCK)
    pad = nb * MOBA_BLOCK - s
    kb = jnp.pad(k, ((0, 0), (0, 0), (0, pad), (0, 0))).reshape(bsz, nh, nb, MOBA_BLOCK, dh)
    vb = jnp.pad(v, ((0, 0), (0, 0), (0, pad), (0, 0))).reshape(bsz, nh, nb, MOBA_BLOCK, dh)
    scale = dh ** -0.5
    pos = jnp.arange(s, dtype=jnp.int32)
    q_block = pos // MOBA_BLOCK
    k_mean = jnp.mean(kb.astype(jnp.float32), axis=3)
    gate = jnp.einsum('bhsd,bhnd->bhsn', q.astype(jnp.float32), k_mean)
    fully_past = jnp.arange(nb)[None, :] < q_block[:, None]
    gate = jnp.where(fully_past, gate, NEG_INF)
    n_sel = min(MOBA_TOPK, nb)
    _, sel_idx = lax.top_k(gate, n_sel)
    sel_valid = sel_idx < q_block[None, None, :, None]
    b_ix = jnp.arange(bsz)[:, None, None, None]
    h_ix = jnp.arange(nh)[None, :, None, None]
    offs = jnp.arange(MOBA_BLOCK, dtype=jnp.int32)
    ql = MOBA_Q_CHUNK

    def chunk(ci):
        start = ci * ql
        qc = lax.dynamic_slice_in_dim(q, start, ql, axis=2)
        idx = lax.dynamic_slice_in_dim(sel_idx, start, ql, axis=2)
        ok = lax.dynamic_slice_in_dim(sel_valid, start, ql, axis=2)
        t = start + jnp.arange(ql, dtype=jnp.int32)
        k_sel = kb[b_ix, h_ix, idx]
        v_sel = vb[b_ix, h_ix, idx]
        own = start // MOBA_BLOCK
        k_own = lax.dynamic_index_in_dim(kb, own, axis=2, keepdims=False)
        v_own = lax.dynamic_index_in_dim(vb, own, axis=2, keepdims=False)
        s_sel = jnp.einsum('bhqd,bhqnkd->bhqnk', qc, k_sel).astype(jnp.float32) * scale
        dist_sel = (t[:, None, None] - (idx[..., None] * MOBA_BLOCK + offs)).astype(jnp.float32)
        s_sel = jnp.where(ok[..., None], s_sel - slopes[:, None, None, None] * dist_sel, NEG_INF)
        dist_own = (t[:, None] - (own * MOBA_BLOCK + offs)[None, :]).astype(jnp.float32)
        s_own = jnp.einsum('bhqd,bhkd->bhqk', qc, k_own).astype(jnp.float32) * scale
        s_own = jnp.where(dist_own >= 0, s_own - slopes[:, None, None] * dist_own, NEG_INF)
        scores = jnp.concatenate([s_sel.reshape(bsz, nh, ql, n_sel * MOBA_BLOCK), s_own], axis=-1)
        p = jax.nn.softmax(scores, axis=-1).astype(v.dtype)
        p_sel = p[..., :n_sel * MOBA_BLOCK].reshape(bsz, nh, ql, n_sel, MOBA_BLOCK)
        p_own = p[..., n_sel * MOBA_BLOCK:]
        return (jnp.einsum('bhqnk,bhqnkd->bhqd', p_sel, v_sel)
                + jnp.einsum('bhqk,bhkd->bhqd', p_own, v_own))

    out = lax.map(chunk, jnp.arange(s // ql))
    return out.transpose(1, 2, 0, 3, 4).reshape(bsz, nh, s, dh)


def _lin_combine(e1, e2):
    a1, b1 = e1
    a2, b2 = e2
    return a1 * a2, a2 * b1 + b2


def rglru(xc, wa, ba, wx, bx, lam):
    bsz, s, width = xc.shape
    xf = xc.astype(jnp.float32)
    xg = xf.reshape(bsz, s, C_BLOCKS, C_BLOCK_DIM)
    r = jax.nn.sigmoid(jnp.einsum('bsgi,gij->bsgj', xg, wa.astype(jnp.float32)).reshape(bsz, s, width) + ba)
    i = jax.nn.sigmoid(jnp.einsum('bsgi,gij->bsgj', xg, wx.astype(jnp.float32)).reshape(bsz, s, width) + bx)
    log_a = -RG_C * r * jax.nn.softplus(-lam.astype(jnp.float32))
    a = jnp.exp(log_a)
    u = jnp.sqrt(-jnp.expm1(2.0 * log_a)) * (i * xf)
    _, hs = lax.associative_scan(_lin_combine, (a, u), axis=1)
    return hs


def mlstm(q, k, v, i_pre, f_pre):
    bsz, nh, s, d = q.shape
    L = MLSTM_CHUNK
    nc = s // L
    k = k * d ** -0.5
    log_f = jax.nn.log_sigmoid(f_pre)
    to_chunks = lambda t: jnp.moveaxis(t.reshape(bsz, nh, nc, L, *t.shape[3:]), 2, 0)
    causal = jnp.tril(jnp.ones((L, L), dtype=bool))

    def step(carry, inp):
        C, n, m = carry
        qc, kc, vc, ic, lfc = inp
        b = jnp.cumsum(lfc, axis=-1)
        dmat = jnp.where(causal, b[..., :, None] - b[..., None, :] + ic[..., None, :], NEG_INF)
        inter = b + m[..., None]
        m_t = jnp.maximum(inter, jnp.max(dmat, axis=-1))
        w_intra = jnp.exp(dmat - m_t[..., None])
        w_inter = jnp.exp(inter - m_t)
        s_qk = jnp.einsum('bhld,bhsd->bhls', qc, kc) * w_intra
        num = (w_inter[..., None] * jnp.einsum('bhld,bhde->bhle', qc, C)
               + jnp.einsum('bhls,bhse->bhle', s_qk, vc))
        den = w_inter * jnp.einsum('bhld,bhd->bhl', qc, n) + jnp.sum(s_qk, axis=-1)
        h = num / jnp.maximum(jnp.abs(den), jnp.exp(-m_t))[..., None]
        b_last = b[..., -1]
        w_s = b_last[..., None] - b + ic
        m_new = jnp.maximum(b_last + m, jnp.max(w_s, axis=-1))
        decay = jnp.exp(b_last + m - m_new)
        ws = jnp.exp(w_s - m_new[..., None])
        C_new = decay[..., None, None] * C + jnp.einsum('bhs,bhsd,bhse->bhde', ws, kc, vc)
        n_new = decay[..., None] * n + jnp.einsum('bhs,bhsd->bhd', ws, kc)
        return (C_new, n_new, m_new), h

    init = (jnp.zeros((bsz, nh, d, d), jnp.float32), jnp.zeros((bsz, nh, d), jnp.float32),
            jnp.zeros((bsz, nh), jnp.float32))
    _, hs = lax.scan(step, init, (to_chunks(q), to_chunks(k), to_chunks(v),
                                  to_chunks(i_pre), to_chunks(log_f)))
    return jnp.moveaxis(hs, 0, 2).reshape(bsz, nh, s, d)


def mixer_ab(h, w_in, conv_w, w_out, slopes):
    bsz, s, _ = h.shape
    proj = h @ w_in
    q = proj[..., 0:A_WIDTH]
    k = proj[..., A_WIDTH:2 * A_WIDTH]
    v = proj[..., 2 * A_WIDTH:3 * A_WIDTH]
    o = 3 * A_WIDTH
    gate_b = proj[..., o:o + B_WIDTH]
    gate_c = proj[..., o + B_WIDTH:o + 2 * B_WIDTH]
    xb = proj[..., o + 2 * B_WIDTH:o + 3 * B_WIDTH]
    heads = lambda t: t.reshape(bsz, s, A_HEADS, A_HEAD_DIM).transpose(0, 2, 1, 3)
    ya = moba_attention(heads(q), heads(k), heads(v), slopes)
    ya = ya.transpose(0, 2, 1, 3).reshape(bsz, s, A_WIDTH)
    yb = gate_b * causal_depthwise_conv(gate_c * xb, conv_w)
    return jnp.concatenate([ya, yb], axis=-1) @ w_out


def mixer_cd(h, w_in, gate_bias, conv_w, conv_b, wa, ba, wx, bx, lam, w_out):
    bsz, s, _ = h.shape
    proj = h @ w_in
    x_rg = proj[..., 0:C_WIDTH]
    g_rg = proj[..., C_WIDTH:2 * C_WIDTH]
    o = 2 * C_WIDTH
    q = proj[..., o:o + D_WIDTH]
    k = proj[..., o + D_WIDTH:o + 2 * D_WIDTH]
    v = proj[..., o + 2 * D_WIDTH:o + 3 * D_WIDTH]
    og = proj[..., o + 3 * D_WIDTH:o + 4 * D_WIDTH]
    gates = (proj[..., o + 4 * D_WIDTH:] + gate_bias).astype(jnp.float32)
    i_pre = gates[..., :D_HEADS].transpose(0, 2, 1)
    f_pre = gates[..., D_HEADS:].transpose(0, 2, 1)
    xc = causal_depthwise_conv(x_rg, conv_w) + conv_b
    yc = rglru(xc, wa, ba, wx, bx, lam).astype(h.dtype) * jax.nn.gelu(g_rg)
    heads = lambda t: t.reshape(bsz, s, D_HEADS, D_HEAD_DIM).transpose(0, 2, 1, 3).astype(jnp.float32)
    hd = mlstm(heads(q), heads(k), heads(v), i_pre, f_pre)
    yd = hd.transpose(0, 2, 1, 3).reshape(bsz, s, D_WIDTH).astype(h.dtype) * jax.nn.sigmoid(og)
    return jnp.concatenate([yc, yd], axis=-1) @ w_out


def swiglu(h, w1, w3, w2):
    return (jax.nn.silu(h @ w1) * (h @ w3)) @ w2


def setup_inputs(seed: int = 0) -> dict:
    key = jax.random.key(seed)
    ks = jax.random.split(key, 24)
    f32 = jnp.float32
    D = D_MODEL

    def nrm(k, shape, scale):
        return jax.random.normal(k, shape, f32) * scale

    i_b = nrm(ks[13], (N_ODD, D_HEADS), 0.1)
    f_b = jnp.linspace(3.0, 6.0, D_HEADS, dtype=f32)[None, :] + nrm(ks[14], (N_ODD, D_HEADS), 0.1)
    u = jax.random.uniform(ks[21], (N_ODD, C_WIDTH), f32, 0.9, 0.999)
    a0 = u ** (1.0 / RG_C)
    return {
        "x": nrm(ks[0], (BATCH, SEQ, D), 1.0),
        "c": nrm(ks[1], (BATCH, D), 1.0),
        "mod_w": nrm(ks[2], (DEPTH, D, 6 * D), 0.5 * D ** -0.5),
        "mod_b": nrm(ks[3], (DEPTH, 6 * D), 0.02),
        "norm_mix_g": 1.0 + nrm(ks[4], (DEPTH, D), 0.02),
        "norm_ffn_g": 1.0 + nrm(ks[5], (DEPTH, D), 0.02),
        "ffn_w1": nrm(ks[6], (DEPTH, D, FFN_HIDDEN), D ** -0.5),
        "ffn_w3": nrm(ks[7], (DEPTH, D, FFN_HIDDEN), D ** -0.5),
        "ffn_w2": nrm(ks[8], (DEPTH, FFN_HIDDEN, D), FFN_HIDDEN ** -0.5),
        "ab_w_in": nrm(ks[9], (N_EVEN, D, AB_IN), D ** -0.5),
        "ab_conv_w": nrm(ks[10], (N_EVEN, B_CONV, B_WIDTH), B_CONV ** -0.5),
        "ab_w_out": nrm(ks[11], (N_EVEN, MIX_WIDTH, D), MIX_WIDTH ** -0.5),
        "cd_w_in": nrm(ks[12], (N_ODD, D, CD_IN), D ** -0.5),
        "cd_gate_b": jnp.concatenate([i_b, f_b], axis=-1),
        "rg_conv_w": nrm(ks[15], (N_ODD, C_CONV, C_WIDTH), C_CONV ** -0.5),
        "rg_conv_b": nrm(ks[16], (N_ODD, C_WIDTH), 0.02),
        "rg_wa": nrm(ks[17], (N_ODD, C_BLOCKS, C_BLOCK_DIM, C_BLOCK_DIM), C_BLOCK_DIM ** -0.5),
        "rg_ba": nrm(ks[18], (N_ODD, C_WIDTH), 0.02),
        "rg_wx": nrm(ks[19], (N_ODD, C_BLOCKS, C_BLOCK_DIM, C_BLOCK_DIM), C_BLOCK_DIM ** -0.5),
        "rg_bx": nrm(ks[20], (N_ODD, C_WIDTH), 0.02),
        "rg_lambda": jnp.log(a0) - jnp.log1p(-a0),
        "cd_w_out": nrm(ks[22], (N_ODD, MIX_WIDTH, D), MIX_WIDTH ** -0.5),
        "final_norm_g": 1.0 + nrm(ks[23], (D,), 0.02),
    }


def reference(x, c, mod_w, mod_b, norm_mix_g, norm_ffn_g, ffn_w1, ffn_w3, ffn_w2,
              ab_w_in, ab_conv_w, ab_w_out, cd_w_in, cd_gate_b, rg_conv_w, rg_conv_b,
              rg_wa, rg_ba, rg_wx, rg_bx, rg_lambda, cd_w_out, final_norm_g):
    slopes = alibi_slopes(A_HEADS)
    cond = jax.nn.silu(c)
    for layer in range(DEPTH):
        mod = cond @ mod_w[layer] + mod_b[layer]
        sh_m, sc_m, g_m, sh_f, sc_f, g_f = jnp.split(mod[:, None, :], 6, axis=-1)
        h = rmsnorm(x, norm_mix_g[layer]) * (1.0 + sc_m) + sh_m
        if layer % 2 == 0:
            e = layer // 2
            y = mixer_ab(h, ab_w_in[e], ab_conv_w[e], ab_w_out[e], slopes)
        else:
            o = layer // 2
            y = mixer_cd(h, cd_w_in[o], cd_gate_b[o], rg_conv_w[o], rg_conv_b[o], rg_wa[o],
                         rg_ba[o], rg_wx[o], rg_bx[o], rg_lambda[o], cd_w_out[o])
        x = x + g_m * y
        h = rmsnorm(x, norm_ffn_g[layer]) * (1.0 + sc_f) + sh_f
        x = x + g_f * swiglu(h, ffn_w1[layer], ffn_w3[layer], ffn_w2[layer])
    return rmsnorm(x, final_norm_g)
```

```python
import functools

import jax
import jax.numpy as jnp
from jax import lax
from jax.experimental import pallas as pl
from jax.experimental.pallas import tpu as pltpu

F32 = jnp.float32
BF16 = jnp.bfloat16

EPS = 1e-6
NEG_INF = -1e30

A_HEAD_DIM = 128
MOBA_BLOCK = 256
MOBA_TOPK = 3
B_CONV = 3
C_BLOCKS = 8
C_CONV = 4
RG_C = 8.0
D_HEADS = 4
MLSTM_CHUNK = 256
GATE_LANES = 128

VMEM_LIMIT_BYTES = 56 * 1024 * 1024
HALO_ROWS = 8


def _cparams(*sem):
    return pltpu.CompilerParams(dimension_semantics=sem, vmem_limit_bytes=VMEM_LIMIT_BYTES)


def _dot(a, b):
    return jnp.dot(a, b, preferred_element_type=F32)


def _dot_nt(a, b):
    return lax.dot_general(a, b, (((1,), (1,)), ((), ())), preferred_element_type=F32)


def _dot_tn(a, b):
    return lax.dot_general(a, b, (((0,), (0,)), ((), ())), preferred_element_type=F32)


def _norm_mod(x, g, scale, shift):
    xn = x * lax.rsqrt(jnp.mean(x * x, axis=-1, keepdims=True) + EPS)
    return (xn * g) * (1.0 + scale) + shift


def _mod_kernel(c_ref, w_ref, b_ref, o_ref):
    c = c_ref[...]
    cond = c * jax.nn.sigmoid(c)
    o_ref[...] = _dot(cond.astype(BF16), w_ref[...].astype(BF16)) + b_ref[...]


def _modulation(c, mod_w, mod_b):
    depth, d, n = mod_w.shape
    bsz = c.shape[0]
    rows = 8
    c8 = jnp.zeros((rows, d), F32).at[:bsz].set(c)
    tn = 1024
    out = pl.pallas_call(
        _mod_kernel,
        grid=(depth, n // tn),
        in_specs=[
            pl.BlockSpec((rows, d), lambda l, j: (0, 0)),
            pl.BlockSpec((None, d, tn), lambda l, j: (l, 0, j)),
            pl.BlockSpec((None, 1, tn), lambda l, j: (l, 0, j)),
        ],
        out_specs=pl.BlockSpec((None, rows, tn), lambda l, j: (l, 0, j)),
        out_shape=jax.ShapeDtypeStruct((depth, rows, n), F32),
        compiler_params=_cparams("arbitrary", "arbitrary"),
        name="adaln_modulation",
    )(c8, mod_w, mod_b.reshape(depth, 1, n))
    return out[:, :bsz].reshape(depth, bsz, 6, d)


def _in_proj_kernel(x_ref, g_ref, mod_ref, w_ref, *rest, shift_row, scale_row, with_gates):
    if with_gates:
        wg_ref, bg_ref, o_ref, og_ref, h_scr = rest
    else:
        o_ref, h_scr = rest
    j = pl.program_id(1)

    @pl.when(j == 0)
    def _():
        h = _norm_mod(x_ref[...], g_ref[...], mod_ref[scale_row:scale_row + 1, :],
                      mod_ref[shift_row:shift_row + 1, :])
        hb = h.astype(BF16)
        h_scr[...] = hb
        if with_gates:
            og_ref[...] = _dot(hb, wg_ref[...]) + bg_ref[...]

    o_ref[...] = _dot(h_scr[...], w_ref[...]).astype(o_ref.dtype)


def _in_proj(x2, g, mod_l, w, seq, *, shift_row, scale_row, w_gate=None, b_gate=None, tm=1024, tn=1024):
    t, d = x2.shape
    n = w.shape[1]
    tiles_per_seq = seq // tm
    with_gates = w_gate is not None
    in_specs = [
        pl.BlockSpec((tm, d), lambda i, j: (i, 0)),
        pl.BlockSpec((1, d), lambda i, j: (0, 0)),
        pl.BlockSpec((None, 6, d), lambda i, j: (i // tiles_per_seq, 0, 0)),
        pl.BlockSpec((d, tn), lambda i, j: (0, j)),
    ]
    args = [x2, g.reshape(1, d), mod_l, w]
    out_specs = pl.BlockSpec((tm, tn), lambda i, j: (i, j))
    out_shape = jax.ShapeDtypeStruct((t, n), BF16)
    if with_gates:
        in_specs += [pl.BlockSpec((d, GATE_LANES), lambda i, j: (0, 0)),
                     pl.BlockSpec((1, GATE_LANES), lambda i, j: (0, 0))]
        args += [w_gate, b_gate]
        out_specs = [out_specs, pl.BlockSpec((tm, GATE_LANES), lambda i, j: (i, 0))]
        out_shape = [out_shape, jax.ShapeDtypeStruct((t, GATE_LANES), F32)]
    return pl.pallas_call(
        functools.partial(_in_proj_kernel, shift_row=shift_row, scale_row=scale_row, with_gates=with_gates),
        grid=(t // tm, n // tn),
        in_specs=in_specs,
        out_specs=out_specs,
        out_shape=out_shape,
        scratch_shapes=[pltpu.VMEM((tm, d), BF16)],
        compiler_params=_cparams("parallel", "arbitrary"),
        name="norm_in_proj_gates" if with_gates else "norm_in_proj",
    )(*args)


def _out_proj_kernel(a_ref, b_ref, w_ref, x_ref, mod_ref, o_ref, *, gate_row, half):
    acc = _dot(a_ref[...], w_ref[0:half, :]) + _dot(b_ref[...], w_ref[half:, :])
    o_ref[...] = x_ref[...] + mod_ref[gate_row:gate_row + 1, :] * acc


def _out_proj(a, b, w, x2, mod_l, seq, *, gate_row, tm=512):
    t, d = x2.shape
    half = a.shape[1]
    tiles_per_seq = seq // tm
    return pl.pallas_call(
        functools.partial(_out_proj_kernel, gate_row=gate_row, half=half),
        grid=(t // tm,),
        in_specs=[
            pl.BlockSpec((tm, half), lambda i: (i, 0)),
            pl.BlockSpec((tm, half), lambda i: (i, 0)),
            pl.BlockSpec((2 * half, d), lambda i: (0, 0)),
            pl.BlockSpec((tm, d), lambda i: (i, 0)),
            pl.BlockSpec((None, 6, d), lambda i: (i // tiles_per_seq, 0, 0)),
        ],
        out_specs=pl.BlockSpec((tm, d), lambda i: (i, 0)),
        out_shape=jax.ShapeDtypeStruct((t, d), F32),
        compiler_params=_cparams("parallel"),
        name="out_proj_residual",
    )(a, b, w, x2, mod_l)


def _ffn_kernel(x_ref, g_ref, mod_ref, w1_ref, w3_ref, w2_ref, *rest, final):
    if final:
        fg_ref, o_ref, h_scr, acc_scr = rest
    else:
        o_ref, h_scr, acc_scr = rest
    j = pl.program_id(1)

    @pl.when(j == 0)
    def _():
        h = _norm_mod(x_ref[...], g_ref[...], mod_ref[4:5, :], mod_ref[3:4, :])
        h_scr[...] = h.astype(BF16)
        acc_scr[...] = jnp.zeros_like(acc_scr)

    h = h_scr[...]
    a = _dot(h, w1_ref[...])
    b = _dot(h, w3_ref[...])
    gated = (a * jax.nn.sigmoid(a)) * b
    acc_scr[...] += _dot(gated.astype(BF16), w2_ref[...])

    @pl.when(j == pl.num_programs(1) - 1)
    def _():
        y = x_ref[...] + mod_ref[5:6, :] * acc_scr[...]
        if final:
            y = y * lax.rsqrt(jnp.mean(y * y, axis=-1, keepdims=True) + EPS) * fg_ref[...]
        o_ref[...] = y


def _ffn(x2, g, mod_l, w1, w3, w2, seq, final_g=None, *, tm=512, tf=512):
    t, d = x2.shape
    f = w1.shape[1]
    tiles_per_seq = seq // tm
    final = final_g is not None
    in_specs = [
        pl.BlockSpec((tm, d), lambda i, j: (i, 0)),
        pl.BlockSpec((1, d), lambda i, j: (0, 0)),
        pl.BlockSpec((None, 6, d), lambda i, j: (i // tiles_per_seq, 0, 0)),
        pl.BlockSpec((d, tf), lambda i, j: (0, j)),
        pl.BlockSpec((d, tf), lambda i, j: (0, j)),
        pl.BlockSpec((tf, d), lambda i, j: (j, 0)),
    ]
    args = [x2, g.reshape(1, d), mod_l, w1, w3, w2]
    if final:
        in_specs.append(pl.BlockSpec((1, d), lambda i, j: (0, 0)))
        args.append(final_g.reshape(1, d))
    return pl.pallas_call(
        functools.partial(_ffn_kernel, final=final),
        grid=(t // tm, f // tf),
        in_specs=in_specs,
        out_specs=pl.BlockSpec((tm, d), lambda i, j: (i, 0)),
        out_shape=jax.ShapeDtypeStruct((t, d), F32),
        scratch_shapes=[pltpu.VMEM((tm, d), BF16), pltpu.VMEM((tm, d), F32)],
        compiler_params=_cparams("parallel", "arbitrary"),
        name="swiglu_ffn_final" if final else "swiglu_ffn",
    )(*args)


def _gated_conv_kernel(gb_ref, gc_ref, xb_ref, gch_ref, xbh_ref, w_ref, o_ref, u_scr, *, tiles_per_seq, tm):
    i = pl.program_id(0)
    u = gc_ref[...].astype(F32) * xb_ref[...].astype(F32)
    uh = gch_ref[...].astype(F32) * xbh_ref[...].astype(F32)
    first = (i % tiles_per_seq) == 0
    u_scr[0:HALO_ROWS, :] = jnp.where(first, 0.0, uh)
    u_scr[HALO_ROWS:, :] = u
    w = w_ref[...]
    conv = w[2:3, :] * u
    conv += w[1:2, :] * u_scr[HALO_ROWS - 1:HALO_ROWS - 1 + tm, :]
    conv += w[0:1, :] * u_scr[HALO_ROWS - 2:HALO_ROWS - 2 + tm, :]
    o_ref[...] = (gb_ref[...].astype(F32) * conv).astype(o_ref.dtype)


def _gated_conv(proj, conv_w, seq, *, col0, tm=1024):
    t = proj.shape[0]
    cw = conv_w.shape[1]
    cb = col0 // cw
    tiles_per_seq = seq // tm
    hb = tm // HALO_ROWS

    def halo_map(k):
        return lambda i: (jnp.maximum(i * hb - 1, 0), cb + k)

    return pl.pallas_call(
        functools.partial(_gated_conv_kernel, tiles_per_seq=tiles_per_seq, tm=tm),
        grid=(t // tm,),
        in_specs=[
            pl.BlockSpec((tm, cw), lambda i: (i, cb)),
            pl.BlockSpec((tm, cw), lambda i: (i, cb + 1)),
            pl.BlockSpec((tm, cw), lambda i: (i, cb + 2)),
            pl.BlockSpec((HALO_ROWS, cw), halo_map(1)),
            pl.BlockSpec((HALO_ROWS, cw), halo_map(2)),
            pl.BlockSpec((B_CONV, cw), lambda i: (0, 0)),
        ],
        out_specs=pl.BlockSpec((tm, cw), lambda i: (i, 0)),
        out_shape=jax.ShapeDtypeStruct((t, cw), BF16),
        scratch_shapes=[pltpu.VMEM((tm + HALO_ROWS, cw), F32)],
        compiler_params=_cparams("parallel"),
        name="gated_short_conv",
    )(proj, proj, proj, proj, proj, conv_w)


def _moba_prep_kernel(q_ref, k_ref, v_ref, bias_ref, vt_ref, *, nb, blk):
    s, dh = q_ref.shape
    kmean = jnp.mean(k_ref[...].astype(F32).reshape(nb, blk, dh), axis=1)
    hi = kmean.astype(BF16)
    lo = (kmean - hi.astype(F32)).astype(BF16)
    q = q_ref[...]
    gate = _dot_nt(hi, q) + _dot_nt(lo, q)
    n_idx = lax.broadcasted_iota(jnp.int32, (nb, s), 0)
    q_blk = lax.broadcasted_iota(jnp.int32, (nb, s), 1) // blk
    eligible = n_idx < q_blk
    g = jnp.where(eligible, gate, NEG_INF)
    rank = jnp.zeros((nb, s), jnp.int32)
    for m in range(nb):
        gm = g[m:m + 1, :]
        beats = (gm > g) | ((gm == g) & (n_idx > m))
        rank += beats.astype(jnp.int32)
    selected = eligible & (rank < MOBA_TOPK)
    bias_ref[...] = jnp.where(selected, 0.0, NEG_INF)
    for n in range(nb):
        vt_ref[n] = v_ref[n * blk:(n + 1) * blk, :].astype(F32).T.astype(vt_ref.dtype)


def _moba_attn_kernel(slopes_ref, q_ref, k_ref, vt_ref, bias_ref, o_ref, *, blk, scale):
    h = pl.program_id(1)
    qi = pl.program_id(2)
    slope = slopes_ref[h]
    q = q_ref[...]
    key_off = lax.broadcasted_iota(jnp.int32, (blk, blk), 0)
    qry_off = lax.broadcasted_iota(jnp.int32, (blk, blk), 1)
    alibi = slope * (qry_off - key_off).astype(F32)

    k_own = k_ref[pl.ds(pl.multiple_of(qi * blk, blk), blk), :]
    s = _dot_nt(k_own, q) * scale
    s = jnp.where(key_off <= qry_off, s - alibi, NEG_INF)
    m0 = jnp.max(s, axis=0, keepdims=True)
    p = jnp.exp(s - m0)
    l0 = jnp.sum(p, axis=0, keepdims=True)
    acc0 = _dot(vt_ref[qi], p.astype(BF16))

    def body(n, carry):
        m, l, acc = carry
        kn = k_ref[pl.ds(pl.multiple_of(n * blk, blk), blk), :]
        row = bias_ref[pl.ds(n, 1), :] - slope * (blk * (qi - n)).astype(F32)
        s = _dot_nt(kn, q) * scale - alibi + row
        m_new = jnp.maximum(m, jnp.max(s, axis=0, keepdims=True))
        alpha = jnp.exp(m - m_new)
        p = jnp.exp(s - m_new)
        l = alpha * l + jnp.sum(p, axis=0, keepdims=True)
        acc = alpha * acc + _dot(vt_ref[n], p.astype(BF16))
        return m_new, l, acc

    _, l, acc = lax.fori_loop(0, qi, body, (m0, l0, acc0))
    o_ref[...] = (acc / l).T.astype(o_ref.dtype)


def _moba(proj3, slopes, n_heads):
    bsz, s, _ = proj3.shape
    dh, blk = A_HEAD_DIM, MOBA_BLOCK
    nb = s // blk
    bias, vt = pl.pallas_call(
        functools.partial(_moba_prep_kernel, nb=nb, blk=blk),
        grid=(bsz, n_heads),
        in_specs=[
            pl.BlockSpec((None, s, dh), lambda b, h: (b, 0, h)),
            pl.BlockSpec((None, s, dh), lambda b, h: (b, 0, n_heads + h)),
            pl.BlockSpec((None, s, dh), lambda b, h: (b, 0, 2 * n_heads + h)),
        ],
        out_specs=[
            pl.BlockSpec((None, None, nb, s), lambda b, h: (b, h, 0, 0)),
            pl.BlockSpec((None, None, nb, dh, blk), lambda b, h: (b, h, 0, 0, 0)),
        ],
        out_shape=[
            jax.ShapeDtypeStruct((bsz, n_heads, nb, s), F32),
            jax.ShapeDtypeStruct((bsz, n_heads, nb, dh, blk), BF16),
        ],
        compiler_params=_cparams("parallel", "parallel"),
        name="moba_select",
    )(proj3, proj3, proj3)
    return pl.pallas_call(
        functools.partial(_moba_attn_kernel, blk=blk, scale=dh ** -0.5),
        grid=(bsz, n_heads, nb),
        in_specs=[
            pl.BlockSpec(memory_space=pltpu.SMEM),
            pl.BlockSpec((None, blk, dh), lambda b, h, i: (b, i, h)),
            pl.BlockSpec((None, s, dh), lambda b, h, i: (b, 0, n_heads + h)),
            pl.BlockSpec((None, None, nb, dh, blk), lambda b, h, i: (b, h, 0, 0, 0)),
            pl.BlockSpec((None, None, nb, blk), lambda b, h, i: (b, h, 0, i)),
        ],
        out_specs=pl.BlockSpec((None, blk, dh), lambda b, h, i: (b, i, h)),
        out_shape=jax.ShapeDtypeStruct((bsz, s, n_heads * dh), BF16),
        compiler_params=_cparams("parallel", "parallel", "arbitrary"),
        name="moba_attention",
    )(slopes, proj3, proj3, vt, bias)


def _shift_rows(x, d, fill):
    n, c = x.shape
    if d % HALO_ROWS == 0:
        return jnp.concatenate([jnp.full((d, c), fill, x.dtype), x[:n - d]], axis=0)
    rolled = pltpu.roll(x, d, axis=0)
    rows = lax.broadcasted_iota(jnp.int32, x.shape, 0)
    return jnp.where(rows < d, fill, rolled)


def _rglru_kernel(x_ref, xh_ref, gr_ref, cw_ref, cb_ref, w_ref, ba_ref, bx_ref, lam_ref, o_ref,
                  x_scr, h_scr, *, ts, n_blocks):
    si = pl.program_id(1)

    @pl.when(si == 0)
    def _():
        h_scr[...] = jnp.zeros_like(h_scr)

    x = x_ref[...].astype(F32)
    x_scr[0:HALO_ROWS, :] = jnp.where(si == 0, 0.0, xh_ref[...].astype(F32))
    x_scr[HALO_ROWS:, :] = x
    cw = cw_ref[...]
    xc = cw[3:4, :] * x + cb_ref[...]
    for j in range(C_CONV - 1):
        off = HALO_ROWS - (C_CONV - 1) + j
        xc += cw[j:j + 1, :] * x_scr[off:off + ts, :]

    bd = xc.shape[1] // n_blocks
    xcb = xc.astype(BF16)
    gates = [_dot(xcb[:, g * bd:(g + 1) * bd], w_ref[g]) for g in range(n_blocks)]
    r = jax.nn.sigmoid(jnp.concatenate([gt[:, :bd] for gt in gates], axis=1) + ba_ref[...])
    i = jax.nn.sigmoid(jnp.concatenate([gt[:, bd:] for gt in gates], axis=1) + bx_ref[...])
    log_a = (-RG_C) * r * jax.nn.softplus(-lam_ref[...])
    a = jnp.exp(log_a)
    u = jnp.sqrt(-jnp.tanh(log_a) * (a * a + 1.0)) * (i * xc)

    d = 1
    while d < ts:
        u = a * _shift_rows(u, d, 0.0) + u
        a = a * _shift_rows(a, d, 1.0)
        d *= 2
    hs = a * h_scr[...] + u
    h_scr[...] = hs[ts - 1:ts, :]
    o_ref[...] = (hs * jax.nn.gelu(gr_ref[...].astype(F32))).astype(o_ref.dtype)


def _rglru(proj3, conv_w, conv_b, w_cat, ba, bx, lam, *, ts=256):
    bsz, s, _ = proj3.shape
    c = conv_w.shape[1]
    n_blocks, bd, _ = w_cat.shape
    hb = ts // HALO_ROWS
    vec = pl.BlockSpec((1, c), lambda b, i: (0, 0))
    return pl.pallas_call(
        functools.partial(_rglru_kernel, ts=ts, n_blocks=n_blocks),
        grid=(bsz, s // ts),
        in_specs=[
            pl.BlockSpec((None, ts, c), lambda b, i: (b, i, 0)),
            pl.BlockSpec((None, HALO_ROWS, c), lambda b, i: (b, jnp.maximum(i * hb - 1, 0), 0)),
            pl.BlockSpec((None, ts, c), lambda b, i: (b, i, 1)),
            pl.BlockSpec((C_CONV, c), lambda b, i: (0, 0)),
            vec,
            pl.BlockSpec((n_blocks, bd, 2 * bd), lambda b, i: (0, 0, 0)),
            vec, vec, vec,
        ],
        out_specs=pl.BlockSpec((None, ts, c), lambda b, i: (b, i, 0)),
        out_shape=jax.ShapeDtypeStruct((bsz, s, c), BF16),
        scratch_shapes=[pltpu.VMEM((ts + HALO_ROWS, c), F32), pltpu.VMEM((1, c), F32)],
        compiler_params=_cparams("parallel", "arbitrary"),
        name="rglru_scan",
    )(proj3, proj3, proj3, conv_w, conv_b.reshape(1, c), w_cat, ba.reshape(1, c), bx.reshape(1, c),
      lam.reshape(1, c))


def _mlstm_kernel(q_ref, k_ref, v_ref, og_ref, gt_ref, o_ref, c_scr, n_scr, m_scr, *, chunk, nh, dh):
    ci = pl.program_id(1)

    @pl.when(ci == 0)
    def _():
        c_scr[...] = jnp.zeros_like(c_scr)
        n_scr[...] = jnp.zeros_like(n_scr)
        m_scr[...] = jnp.zeros_like(m_scr)

    gates = gt_ref[...]
    bcum = jax.nn.log_sigmoid(gates)
    d = 1
    while d < chunk:
        bcum = bcum + _shift_rows(bcum, d, 0.0)
        d *= 2
    lane = lax.broadcasted_iota(jnp.int32, gates.shape, 1)
    rows_t = jnp.where(lane < nh, gates, bcum).T
    r_idx = lax.broadcasted_iota(jnp.int32, (chunk, chunk), 0)
    c_idx = lax.broadcasted_iota(jnp.int32, (chunk, chunk), 1)
    causal = c_idx <= r_idx
    k_scale = dh ** -0.5

    for h in range(nh):
        cols = slice(h * dh, (h + 1) * dh)
        i_col = gates[:, h:h + 1]
        b_col = bcum[:, nh + h:nh + h + 1]
        i_row = rows_t[h:h + 1, :]
        b_row = rows_t[nh + h:nh + h + 1, :]
        m_prev = m_scr[h]

        dmat = jnp.where(causal, b_col - b_row + i_row, NEG_INF)
        inter = b_col + m_prev
        m_t = jnp.maximum(inter, jnp.max(dmat, axis=-1, keepdims=True))
        w_intra = jnp.exp(dmat - m_t)
        w_inter = jnp.exp(inter - m_t)

        q = q_ref[:, cols]
        ks = k_ref[:, cols].astype(F32) * k_scale
        ksb = ks.astype(BF16)
        v = v_ref[:, cols]
        s_qk = _dot_nt(q, ksb) * w_intra
        c_prev = c_scr[h]
        n_prev = n_scr[h]
        num = w_inter * _dot(q, c_prev.astype(BF16)) + _dot(s_qk.astype(BF16), v)
        q_n = jnp.sum(q.astype(F32) * n_prev, axis=-1, keepdims=True)
        den = w_inter * q_n + jnp.sum(s_qk, axis=-1, keepdims=True)
        hid = num / jnp.maximum(jnp.abs(den), jnp.exp(-m_t))

        b_last = b_col[chunk - 1:chunk, :]
        w_s = b_last - b_col + i_col
        m_new = jnp.maximum(b_last + m_prev, jnp.max(w_s, axis=0, keepdims=True))
        decay = jnp.exp(b_last + m_prev - m_new)
        kw = jnp.exp(w_s - m_new) * ks
        c_scr[h] = decay * c_prev + _dot_tn(kw.astype(BF16), v)
        n_scr[h] = decay * n_prev + jnp.sum(kw, axis=0, keepdims=True)
        m_scr[h] = m_new

        o_ref[:, cols] = (hid * jax.nn.sigmoid(og_ref[:, cols].astype(F32))).astype(o_ref.dtype)


def _mlstm(proj3, gates3, *, col0, nh, chunk=MLSTM_CHUNK):
    bsz, s, _ = proj3.shape
    dh = 256
    w = nh * dh
    cb = col0 // w
    blk = lambda k: pl.BlockSpec((None, chunk, w), lambda b, i: (b, i, cb + k))
    return pl.pallas_call(
        functools.partial(_mlstm_kernel, chunk=chunk, nh=nh, dh=dh),
        grid=(bsz, s // chunk),
        in_specs=[blk(0), blk(1), blk(2), blk(3),
                  pl.BlockSpec((None, chunk, GATE_LANES), lambda b, i: (b, i, 0))],
        out_specs=pl.BlockSpec((None, chunk, w), lambda b, i: (b, i, 0)),
        out_shape=jax.ShapeDtypeStruct((bsz, s, w), BF16),
        scratch_shapes=[pltpu.VMEM((nh, dh, dh), F32), pltpu.VMEM((nh, 1, dh), F32),
                        pltpu.VMEM((nh, 1, 1), F32)],
        compiler_params=_cparams("parallel", "arbitrary"),
        name="mlstm_chunkwise",
    )(proj3, proj3, proj3, proj3, gates3)


def kernel(x, c, mod_w, mod_b, norm_mix_g, norm_ffn_g, ffn_w1, ffn_w3, ffn_w2, ab_w_in, ab_conv_w, ab_w_out, cd_w_in, cd_gate_b, rg_conv_w, rg_conv_b, rg_wa, rg_ba, rg_wx, rg_bx, rg_lambda, cd_w_out, final_norm_g):
    bsz, seq, d = x.shape
    depth = mod_w.shape[0]
    t = bsz * seq
    half = d // 2
    a_heads = half // A_HEAD_DIM
    slopes = 2.0 ** (-8.0 * jnp.arange(1, a_heads + 1, dtype=F32) / a_heads)

    mod = _modulation(c, mod_w, mod_b)
    x2 = x.reshape(t, d)
    for layer in range(depth):
        mod_l = mod[layer]
        if layer % 2 == 0:
            e = layer // 2
            proj = _in_proj(x2, norm_mix_g[layer], mod_l, ab_w_in[e].astype(BF16), seq,
                            shift_row=0, scale_row=1)
            proj3 = proj.reshape(bsz, seq, proj.shape[1])
            y_a = _moba(proj3, slopes, a_heads).reshape(t, half)
            y_b = _gated_conv(proj, ab_conv_w[e], seq, col0=3 * half)
            w_out = ab_w_out[e]
        else:
            o = layer // 2
            w_in = cd_w_in[o]
            n_main = 2 * half + 4 * half
            n_g = w_in.shape[1] - n_main
            w_gate = jnp.zeros((d, GATE_LANES), BF16).at[:, :n_g].set(w_in[:, n_main:].astype(BF16))
            b_gate = jnp.zeros((1, GATE_LANES), F32).at[0, :n_g].set(cd_gate_b[o])
            proj, gates = _in_proj(x2, norm_mix_g[layer], mod_l, w_in[:, :n_main].astype(BF16), seq,
                                   shift_row=0, scale_row=1, w_gate=w_gate, b_gate=b_gate)
            proj3 = proj.reshape(bsz, seq, n_main)
            w_cat = jnp.concatenate([rg_wa[o], rg_wx[o]], axis=-1).astype(BF16)
            y_a = _rglru(proj3, rg_conv_w[o], rg_conv_b[o], w_cat, rg_ba[o], rg_bx[o],
                         rg_lambda[o]).reshape(t, half)
            y_b = _mlstm(proj3, gates.reshape(bsz, seq, GATE_LANES), col0=2 * half,
                         nh=D_HEADS).reshape(t, half)
            w_out = cd_w_out[o]
        x2 = _out_proj(y_a, y_b, w_out.astype(BF16), x2, mod_l, seq, gate_row=2)
        final_g = final_norm_g if layer == depth - 1 else None
        x2 = _ffn(x2, norm_ffn_g[layer], mod_l, ffn_w1[layer].astype(BF16), ffn_w3[layer].astype(BF16),
                  ffn_w2[layer].astype(BF16), seq, final_g)
    return x2.reshape(bsz, seq, d)
```

```python
import functools

import jax
import jax.numpy as jnp
from jax import lax
from jax.experimental import pallas as pl
from jax.experimental.pallas import tpu as pltpu

F32 = jnp.float32
BF16 = jnp.bfloat16

EPS = 1e-6
NEG_INF = -1e30
LOG2E = 1.4426950408889634

A_HEAD_DIM = 128
MOBA_BLOCK = 256
MOBA_TOPK = 3
MOBA_GROUP = 4
B_CONV = 3
C_BLOCKS = 8
C_CONV = 4
RG_C = 8.0
D_HEADS = 4
MLSTM_CHUNK = 256
GATE_LANES = 128

VMEM_LIMIT_BYTES = 56 * 1024 * 1024
HALO_ROWS = 8
NORM_CHUNKS = 4


def _cparams(*sem):
    return pltpu.CompilerParams(dimension_semantics=sem, vmem_limit_bytes=VMEM_LIMIT_BYTES)


def _dot(a, b):
    return jnp.dot(a, b, preferred_element_type=F32)


def _dot_nt(a, b):
    return lax.dot_general(a, b, (((1,), (1,)), ((), ())), preferred_element_type=F32)


def _dot_tn(a, b):
    return lax.dot_general(a, b, (((0,), (0,)), ((), ())), preferred_element_type=F32)


def _norm_mod(x, g, scale, shift):
    xn = x * lax.rsqrt(jnp.mean(x * x, axis=-1, keepdims=True) + EPS)
    return (xn * g) * (1.0 + scale) + shift


def _mod_kernel(c_ref, w_ref, b_ref, o_ref):
    c = c_ref[...]
    cond = c * jax.nn.sigmoid(c)
    o_ref[...] = _dot(cond.astype(BF16), w_ref[...].astype(BF16)) + b_ref[...]


def _modulation(c, mod_w, mod_b):
    depth, d, n = mod_w.shape
    bsz = c.shape[0]
    rows = 8
    c8 = jnp.zeros((rows, d), F32).at[:bsz].set(c)
    tn = 1024
    out = pl.pallas_call(
        _mod_kernel,
        grid=(depth, n // tn),
        in_specs=[
            pl.BlockSpec((rows, d), lambda l, j: (0, 0)),
            pl.BlockSpec((None, d, tn), lambda l, j: (l, 0, j)),
            pl.BlockSpec((None, 1, tn), lambda l, j: (l, 0, j)),
        ],
        out_specs=pl.BlockSpec((None, rows, tn), lambda l, j: (l, 0, j)),
        out_shape=jax.ShapeDtypeStruct((depth, rows, n), F32),
        compiler_params=_cparams("arbitrary", "arbitrary"),
        name="adaln_modulation",
    )(c8, mod_w, mod_b.reshape(depth, 1, n))
    return out[:, :bsz].reshape(depth, bsz, 6, d)


def _in_proj_kernel(x_ref, g_ref, mod_ref, w_ref, *rest, shift_row, scale_row, with_gates):
    if with_gates:
        wg_ref, bg_ref, o_ref, og_ref, h_scr = rest
    else:
        o_ref, h_scr = rest
    j = pl.program_id(1)

    @pl.when(j == 0)
    def _():
        rc = x_ref.shape[0] // NORM_CHUNKS
        for r in range(NORM_CHUNKS):
            rows = slice(r * rc, (r + 1) * rc)
            h = _norm_mod(x_ref[rows, :], g_ref[...], mod_ref[scale_row:scale_row + 1, :],
                          mod_ref[shift_row:shift_row + 1, :])
            hb = h.astype(BF16)
            h_scr[rows, :] = hb
            o_ref[rows, :] = _dot(hb, w_ref[...]).astype(o_ref.dtype)
            if with_gates:
                og_ref[rows, :] = _dot(hb, wg_ref[...]) + bg_ref[...]

    @pl.when(j > 0)
    def _():
        o_ref[...] = _dot(h_scr[...], w_ref[...]).astype(o_ref.dtype)


def _in_proj(x2, g, mod_l, w, seq, *, shift_row, scale_row, w_gate=None, b_gate=None, tm=1024, tn=1024):
    t, d = x2.shape
    n = w.shape[1]
    tiles_per_seq = seq // tm
    with_gates = w_gate is not None
    in_specs = [
        pl.BlockSpec((tm, d), lambda i, j: (i, 0)),
        pl.BlockSpec((1, d), lambda i, j: (0, 0)),
        pl.BlockSpec((None, 6, d), lambda i, j: (i // tiles_per_seq, 0, 0)),
        pl.BlockSpec((d, tn), lambda i, j: (0, j)),
    ]
    args = [x2, g.reshape(1, d), mod_l, w]
    out_specs = pl.BlockSpec((tm, tn), lambda i, j: (i, j))
    out_shape = jax.ShapeDtypeStruct((t, n), BF16)
    if with_gates:
        in_specs += [pl.BlockSpec((d, GATE_LANES), lambda i, j: (0, 0)),
                     pl.BlockSpec((1, GATE_LANES), lambda i, j: (0, 0))]
        args += [w_gate, b_gate]
        out_specs = [out_specs, pl.BlockSpec((tm, GATE_LANES), lambda i, j: (i, 0))]
        out_shape = [out_shape, jax.ShapeDtypeStruct((t, GATE_LANES), F32)]
    return pl.pallas_call(
        functools.partial(_in_proj_kernel, shift_row=shift_row, scale_row=scale_row, with_gates=with_gates),
        grid=(t // tm, n // tn),
        in_specs=in_specs,
        out_specs=out_specs,
        out_shape=out_shape,
        scratch_shapes=[pltpu.VMEM((tm, d), BF16)],
        compiler_params=_cparams("parallel", "arbitrary"),
        name="norm_in_proj_gates" if with_gates else "norm_in_proj",
    )(*args)


def _out_proj_kernel(a_ref, b_ref, w_ref, x_ref, mod_ref, o_ref, *, gate_row, half):
    acc = _dot(a_ref[...], w_ref[0:half, :]) + _dot(b_ref[...], w_ref[half:, :])
    o_ref[...] = x_ref[...] + mod_ref[gate_row:gate_row + 1, :] * acc


def _out_proj(a, b, w, x2, mod_l, seq, *, gate_row, tm=512):
    t, d = x2.shape
    half = a.shape[1]
    tiles_per_seq = seq // tm
    return pl.pallas_call(
        functools.partial(_out_proj_kernel, gate_row=gate_row, half=half),
        grid=(t // tm,),
        in_specs=[
            pl.BlockSpec((tm, half), lambda i: (i, 0)),
            pl.BlockSpec((tm, half), lambda i: (i, 0)),
            pl.BlockSpec((2 * half, d), lambda i: (0, 0)),
            pl.BlockSpec((tm, d), lambda i: (i, 0)),
            pl.BlockSpec((None, 6, d), lambda i: (i // tiles_per_seq, 0, 0)),
        ],
        out_specs=pl.BlockSpec((tm, d), lambda i: (i, 0)),
        out_shape=jax.ShapeDtypeStruct((t, d), F32),
        compiler_params=_cparams("parallel"),
        name="out_proj_residual",
    )(a, b, w, x2, mod_l)


def _ffn_kernel(x_ref, g_ref, mod_ref, w1_ref, w3_ref, w2_ref, *rest, final):
    if final:
        fg_ref, o_ref, h_scr, acc_scr = rest
    else:
        o_ref, h_scr, acc_scr = rest
    j = pl.program_id(1)

    def hidden_tile(h):
        a = _dot(h, w1_ref[...])
        b = _dot(h, w3_ref[...])
        gated = (a * jax.nn.sigmoid(a)) * b
        return _dot(gated.astype(BF16), w2_ref[...])

    @pl.when(j == 0)
    def _():
        rc = x_ref.shape[0] // NORM_CHUNKS
        for r in range(NORM_CHUNKS):
            rows = slice(r * rc, (r + 1) * rc)
            h = _norm_mod(x_ref[rows, :], g_ref[...], mod_ref[4:5, :], mod_ref[3:4, :]).astype(BF16)
            h_scr[rows, :] = h
            acc_scr[rows, :] = hidden_tile(h)

    @pl.when(j > 0)
    def _():
        acc_scr[...] += hidden_tile(h_scr[...])

    @pl.when(j == pl.num_programs(1) - 1)
    def _():
        y = x_ref[...] + mod_ref[5:6, :] * acc_scr[...]
        if final:
            y = y * lax.rsqrt(jnp.mean(y * y, axis=-1, keepdims=True) + EPS) * fg_ref[...]
        o_ref[...] = y


def _ffn(x2, g, mod_l, w1, w3, w2, seq, final_g=None, *, tm=512, tf=512):
    t, d = x2.shape
    f = w1.shape[1]
    tiles_per_seq = seq // tm
    final = final_g is not None
    in_specs = [
        pl.BlockSpec((tm, d), lambda i, j: (i, 0)),
        pl.BlockSpec((1, d), lambda i, j: (0, 0)),
        pl.BlockSpec((None, 6, d), lambda i, j: (i // tiles_per_seq, 0, 0)),
        pl.BlockSpec((d, tf), lambda i, j: (0, j)),
        pl.BlockSpec((d, tf), lambda i, j: (0, j)),
        pl.BlockSpec((tf, d), lambda i, j: (j, 0)),
    ]
    args = [x2, g.reshape(1, d), mod_l, w1, w3, w2]
    if final:
        in_specs.append(pl.BlockSpec((1, d), lambda i, j: (0, 0)))
        args.append(final_g.reshape(1, d))
    return pl.pallas_call(
        functools.partial(_ffn_kernel, final=final),
        grid=(t // tm, f // tf),
        in_specs=in_specs,
        out_specs=pl.BlockSpec((tm, d), lambda i, j: (i, 0)),
        out_shape=jax.ShapeDtypeStruct((t, d), F32),
        scratch_shapes=[pltpu.VMEM((tm, d), BF16), pltpu.VMEM((tm, d), F32)],
        compiler_params=_cparams("parallel", "arbitrary"),
        name="swiglu_ffn_final" if final else "swiglu_ffn",
    )(*args)


def _gated_conv_kernel(gb_ref, gc_ref, xb_ref, gch_ref, xbh_ref, w_ref, o_ref, u_scr, *, tiles_per_seq, tm):
    i = pl.program_id(0)
    u = gc_ref[...].astype(F32) * xb_ref[...].astype(F32)
    uh = gch_ref[...].astype(F32) * xbh_ref[...].astype(F32)
    first = (i % tiles_per_seq) == 0
    u_scr[0:HALO_ROWS, :] = jnp.where(first, 0.0, uh)
    u_scr[HALO_ROWS:, :] = u
    w = w_ref[...]
    conv = w[2:3, :] * u
    conv += w[1:2, :] * u_scr[HALO_ROWS - 1:HALO_ROWS - 1 + tm, :]
    conv += w[0:1, :] * u_scr[HALO_ROWS - 2:HALO_ROWS - 2 + tm, :]
    o_ref[...] = (gb_ref[...].astype(F32) * conv).astype(o_ref.dtype)


def _gated_conv(proj, conv_w, seq, *, col0, tm=1024):
    t = proj.shape[0]
    cw = conv_w.shape[1]
    cb = col0 // cw
    tiles_per_seq = seq // tm
    hb = tm // HALO_ROWS

    def halo_map(k):
        return lambda i: (jnp.maximum(i * hb - 1, 0), cb + k)

    return pl.pallas_call(
        functools.partial(_gated_conv_kernel, tiles_per_seq=tiles_per_seq, tm=tm),
        grid=(t // tm,),
        in_specs=[
            pl.BlockSpec((tm, cw), lambda i: (i, cb)),
            pl.BlockSpec((tm, cw), lambda i: (i, cb + 1)),
            pl.BlockSpec((tm, cw), lambda i: (i, cb + 2)),
            pl.BlockSpec((HALO_ROWS, cw), halo_map(1)),
            pl.BlockSpec((HALO_ROWS, cw), halo_map(2)),
            pl.BlockSpec((B_CONV, cw), lambda i: (0, 0)),
        ],
        out_specs=pl.BlockSpec((tm, cw), lambda i: (i, 0)),
        out_shape=jax.ShapeDtypeStruct((t, cw), BF16),
        scratch_shapes=[pltpu.VMEM((tm + HALO_ROWS, cw), F32)],
        compiler_params=_cparams("parallel"),
        name="gated_short_conv",
    )(proj, proj, proj, proj, proj, conv_w)


def _split3(x):
    hi = x.astype(BF16).astype(F32)
    mid = (x - hi).astype(BF16).astype(F32)
    lo = (x - hi - mid).astype(BF16).astype(F32)
    return hi, mid, lo


def _moba_prep_kernel(slopes_ref, q_ref, k_ref, v_ref, qx_ref, kx_ref, vt_ref, *, nb, blk, grp):
    s, dh = q_ref.shape
    u = slopes_ref[pl.program_id(1)] * (dh ** 0.5)
    kmean = jnp.mean(k_ref[...].astype(F32).reshape(nb, blk, dh), axis=1)
    hi = kmean.astype(BF16)
    lo = (kmean - hi.astype(F32)).astype(BF16)
    q = q_ref[...]
    gate = _dot_nt(hi, q) + _dot_nt(lo, q)
    n_idx = lax.broadcasted_iota(jnp.int32, (nb, s), 0)
    q_blk = lax.broadcasted_iota(jnp.int32, (nb, s), 1) // blk
    eligible = n_idx < q_blk
    g = jnp.where(eligible, gate, NEG_INF)
    rank = jnp.zeros((nb, s), jnp.int32)
    for m in range(nb):
        gm = g[m:m + 1, :]
        beats = (gm > g) | ((gm == g) & (n_idx > m))
        rank += beats.astype(jnp.int32)
    selected = eligible & (rank < MOBA_TOPK)
    sel_bias = jnp.where(selected | (n_idx == q_blk), 0.0, NEG_INF)

    off = (lax.broadcasted_iota(jnp.int32, (8, s), 1) % blk).astype(F32) * u
    hi, mid, lo = _split3(off)
    feat = lax.broadcasted_iota(jnp.int32, (8, s), 0)
    pos_rows = jnp.where(feat == 0, hi, jnp.where(feat == 1, mid, jnp.where(feat == 2, lo,
                         jnp.where(feat < 6, 1.0, 0.0))))
    ext_t = jnp.concatenate([sel_bias, pos_rows, jnp.zeros((dh - nb - 8, s), F32)], axis=0)
    chunk = grp * blk
    for ci in range(s // chunk):
        rows = slice(ci * chunk, (ci + 1) * chunk)
        qx_ref[rows, 0:dh] = q_ref[rows, :]
        qx_ref[rows, dh:2 * dh] = ext_t[:, rows].T.astype(qx_ref.dtype)

    col = lax.broadcasted_iota(jnp.int32, (s, dh), 1)
    row = lax.broadcasted_iota(jnp.int32, (s, dh), 0)
    khi, kmid, klo = _split3((row % blk).astype(F32) * u)
    k_ext = jnp.where(col < nb, (col == row // blk).astype(F32),
                      jnp.where(col < nb + 3, -1.0,
                                jnp.where(col == nb + 3, khi, jnp.where(col == nb + 4, kmid,
                                          jnp.where(col == nb + 5, klo, 0.0)))))
    kx_ref[:, 0:dh] = k_ref[...]
    kx_ref[:, dh:2 * dh] = k_ext.astype(kx_ref.dtype)

    for n in range(nb):
        v_t = v_ref[n * blk:(n + 1) * blk, :].astype(F32).T.astype(vt_ref.dtype)
        vt_ref[n // grp, :, (n % grp) * blk:(n % grp + 1) * blk] = v_t


def _moba_attn_kernel(slopes_ref, qx_ref, kx_ref, vt_ref, o_ref, m_scr, l_scr, acc_scr, *, blk, grp, scale):
    h = pl.program_id(1)
    qt = pl.program_id(2)
    slope2 = slopes_ref[h] * LOG2E
    c2 = scale * LOG2E
    causal = (lax.broadcasted_iota(jnp.int32, (blk, blk), 0)
              <= lax.broadcasted_iota(jnp.int32, (blk, blk), 1))

    def scores(b, kg):
        return _dot_nt(kg, qx_ref[b * blk:(b + 1) * blk, :])

    def update(b, raw, vt_g, n0, n_sub, first):
        q_blk = qt * grp + b
        subs, shifts = [], []
        m_new = None if first else m_scr[b]
        for gi in range(n_sub):
            r = raw[gi * blk:(gi + 1) * blk, :]
            if first and gi == b:
                r = jnp.where(causal, r, NEG_INF)
            shift = -slope2 * (blk * (q_blk - (n0 + gi))).astype(F32)
            cm = jnp.max(r, axis=0, keepdims=True) * c2 + shift
            m_new = cm if m_new is None else jnp.maximum(m_new, cm)
            subs.append(r)
            shifts.append(shift)
        l = None
        probs = []
        for r, shift in zip(subs, shifts):
            p = jnp.exp2(r * c2 - (m_new - shift))
            ps = jnp.sum(p, axis=0, keepdims=True)
            l = ps if l is None else l + ps
            probs.append(p.astype(BF16))
        pv = _dot(vt_g, jnp.concatenate(probs, axis=0) if n_sub > 1 else probs[0])
        if first:
            acc_scr[b] = pv
        else:
            alpha = jnp.exp2(m_scr[b] - m_new)
            l = alpha * l_scr[b] + l
            acc_scr[b] = alpha * acc_scr[b] + pv
        m_scr[b] = m_new
        l_scr[b] = l


    base = pl.multiple_of(qt * (grp * blk), grp * blk)
    raws = [scores(b, kx_ref[pl.ds(base, (b + 1) * blk), :]) for b in range(grp)]
    for b in range(grp):
        update(b, raws[b], vt_ref[qt, :, 0:(b + 1) * blk], qt * grp, b + 1, True)

    def body(g, carry):
        kg = kx_ref[pl.ds(pl.multiple_of(g * (grp * blk), grp * blk), grp * blk), :]
        vt_g = vt_ref[g]
        raws = [scores(b, kg) for b in range(grp)]
        for b in range(grp):
            update(b, raws[b], vt_g, g * grp, grp, False)
        return carry

    lax.fori_loop(0, qt, body, 0)
    for b in range(grp):
        o_ref[b * blk:(b + 1) * blk, :] = (acc_scr[b] / l_scr[b]).T.astype(o_ref.dtype)


def _moba(proj3, slopes, n_heads):
    bsz, s, _ = proj3.shape
    dh, blk, grp = A_HEAD_DIM, MOBA_BLOCK, MOBA_GROUP
    nb = s // blk
    ng = nb // grp
    tq = grp * blk
    smem = pl.BlockSpec(memory_space=pltpu.SMEM)
    qx, kx, vt = pl.pallas_call(
        functools.partial(_moba_prep_kernel, nb=nb, blk=blk, grp=grp),
        grid=(bsz, n_heads),
        in_specs=[
            smem,
            pl.BlockSpec((None, s, dh), lambda b, h: (b, 0, h)),
            pl.BlockSpec((None, s, dh), lambda b, h: (b, 0, n_heads + h)),
            pl.BlockSpec((None, s, dh), lambda b, h: (b, 0, 2 * n_heads + h)),
        ],
        out_specs=[
            pl.BlockSpec((None, None, s, 2 * dh), lambda b, h: (b, h, 0, 0)),
            pl.BlockSpec((None, None, s, 2 * dh), lambda b, h: (b, h, 0, 0)),
            pl.BlockSpec((None, None, ng, dh, tq), lambda b, h: (b, h, 0, 0, 0)),
        ],
        out_shape=[
            jax.ShapeDtypeStruct((bsz, n_heads, s, 2 * dh), BF16),
            jax.ShapeDtypeStruct((bsz, n_heads, s, 2 * dh), BF16),
            jax.ShapeDtypeStruct((bsz, n_heads, ng, dh, tq), BF16),
        ],
        compiler_params=_cparams("parallel", "parallel"),
        name="moba_select",
    )(slopes, proj3, proj3, proj3)
    return pl.pallas_call(
        functools.partial(_moba_attn_kernel, blk=blk, grp=grp, scale=dh ** -0.5),
        grid=(bsz, n_heads, ng),
        in_specs=[
            smem,
            pl.BlockSpec((None, None, tq, 2 * dh), lambda b, h, i: (b, h, i, 0)),
            pl.BlockSpec((None, None, s, 2 * dh), lambda b, h, i: (b, h, 0, 0)),
            pl.BlockSpec((None, None, ng, dh, tq), lambda b, h, i: (b, h, 0, 0, 0)),
        ],
        out_specs=pl.BlockSpec((None, tq, dh), lambda b, h, i: (b, i, h)),
        out_shape=jax.ShapeDtypeStruct((bsz, s, n_heads * dh), BF16),
        scratch_shapes=[pltpu.VMEM((grp, 1, blk), F32), pltpu.VMEM((grp, 1, blk), F32),
                        pltpu.VMEM((grp, dh, blk), F32)],
        compiler_params=_cparams("parallel", "parallel", "arbitrary"),
        name="moba_attention",
    )(slopes, qx, kx, vt)


def _shift_rows(x, d, fill):
    n, c = x.shape
    if d % HALO_ROWS == 0:
        return jnp.concatenate([jnp.full((d, c), fill, x.dtype), x[:n - d]], axis=0)
    rolled = pltpu.roll(x, d, axis=0)
    rows = lax.broadcasted_iota(jnp.int32, x.shape, 0)
    return jnp.where(rows < d, fill, rolled)


def _rglru_kernel(x_ref, xh_ref, gr_ref, cw_ref, cb_ref, w_ref, ba_ref, bx_ref, lam_ref, o_ref,
                  x_scr, h_scr, *, ts, n_blocks):
    si = pl.program_id(1)

    @pl.when(si == 0)
    def _():
        h_scr[...] = jnp.zeros_like(h_scr)

    x = x_ref[...].astype(F32)
    x_scr[0:HALO_ROWS, :] = jnp.where(si == 0, 0.0, xh_ref[...].astype(F32))
    x_scr[HALO_ROWS:, :] = x
    cw = cw_ref[...]
    xc = cw[3:4, :] * x + cb_ref[...]
    for j in range(C_CONV - 1):
        off = HALO_ROWS - (C_CONV - 1) + j
        xc += cw[j:j + 1, :] * x_scr[off:off + ts, :]

    bd = xc.shape[1] // n_blocks
    xcb = xc.astype(BF16)
    gates = [_dot(xcb[:, g * bd:(g + 1) * bd], w_ref[g]) for g in range(n_blocks)]
    r = jax.nn.sigmoid(jnp.concatenate([gt[:, :bd] for gt in gates], axis=1) + ba_ref[...])
    i = jax.nn.sigmoid(jnp.concatenate([gt[:, bd:] for gt in gates], axis=1) + bx_ref[...])
    log_a = (-RG_C) * r * jax.nn.softplus(-lam_ref[...])
    a = jnp.exp(log_a)
    u = jnp.sqrt(-jnp.tanh(log_a) * (a * a + 1.0)) * (i * xc)

    d = 1
    while d < ts:
        u = a * _shift_rows(u, d, 0.0) + u
        a = a * _shift_rows(a, d, 1.0)
        d *= 2
    hs = a * h_scr[...] + u
    h_scr[...] = hs[ts - 1:ts, :]
    o_ref[...] = (hs * jax.nn.gelu(gr_ref[...].astype(F32))).astype(o_ref.dtype)


def _rglru(proj3, conv_w, conv_b, w_cat, ba, bx, lam, *, ts=256):
    bsz, s, _ = proj3.shape
    c = conv_w.shape[1]
    n_blocks, bd, _ = w_cat.shape
    hb = ts // HALO_ROWS
    vec = pl.BlockSpec((1, c), lambda b, i: (0, 0))
    return pl.pallas_call(
        functools.partial(_rglru_kernel, ts=ts, n_blocks=n_blocks),
        grid=(bsz, s // ts),
        in_specs=[
            pl.BlockSpec((None, ts, c), lambda b, i: (b, i, 0)),
            pl.BlockSpec((None, HALO_ROWS, c), lambda b, i: (b, jnp.maximum(i * hb - 1, 0), 0)),
            pl.BlockSpec((None, ts, c), lambda b, i: (b, i, 1)),
            pl.BlockSpec((C_CONV, c), lambda b, i: (0, 0)),
            vec,
            pl.BlockSpec((n_blocks, bd, 2 * bd), lambda b, i: (0, 0, 0)),
            vec, vec, vec,
        ],
        out_specs=pl.BlockSpec((None, ts, c), lambda b, i: (b, i, 0)),
        out_shape=jax.ShapeDtypeStruct((bsz, s, c), BF16),
        scratch_shapes=[pltpu.VMEM((ts + HALO_ROWS, c), F32), pltpu.VMEM((1, c), F32)],
        compiler_params=_cparams("parallel", "arbitrary"),
        name="rglru_scan",
    )(proj3, proj3, proj3, conv_w, conv_b.reshape(1, c), w_cat, ba.reshape(1, c), bx.reshape(1, c),
      lam.reshape(1, c))


def _mlstm_kernel(q_ref, k_ref, v_ref, og_ref, gt_ref, o_ref, c_scr, n_scr, m_scr, *, chunk, nh, dh):
    ci = pl.program_id(1)

    @pl.when(ci == 0)
    def _():
        c_scr[...] = jnp.zeros_like(c_scr)
        n_scr[...] = jnp.zeros_like(n_scr)
        m_scr[...] = jnp.zeros_like(m_scr)

    gates = gt_ref[...]
    bcum = jax.nn.log_sigmoid(gates)
    d = 1
    while d < chunk:
        bcum = bcum + _shift_rows(bcum, d, 0.0)
        d *= 2
    lane = lax.broadcasted_iota(jnp.int32, gates.shape, 1)
    rows_t = jnp.where(lane < nh, gates, bcum).T
    r_idx = lax.broadcasted_iota(jnp.int32, (chunk, chunk), 0)
    c_idx = lax.broadcasted_iota(jnp.int32, (chunk, chunk), 1)
    causal = c_idx <= r_idx
    k_scale = dh ** -0.5

    for h in range(nh):
        cols = slice(h * dh, (h + 1) * dh)
        i_col = gates[:, h:h + 1]
        b_col = bcum[:, nh + h:nh + h + 1]
        i_row = rows_t[h:h + 1, :]
        b_row = rows_t[nh + h:nh + h + 1, :]
        m_prev = m_scr[h]

        dmat = jnp.where(causal, b_col - b_row + i_row, NEG_INF)
        inter = b_col + m_prev
        m_t = jnp.maximum(inter, jnp.max(dmat, axis=-1, keepdims=True))
        w_intra = jnp.exp(dmat - m_t)
        w_inter = jnp.exp(inter - m_t)

        q = q_ref[:, cols]
        ks = k_ref[:, cols].astype(F32) * k_scale
        ksb = ks.astype(BF16)
        v = v_ref[:, cols]
        s_qk = _dot_nt(q, ksb) * w_intra
        c_prev = c_scr[h]
        n_prev = n_scr[h]
        num = w_inter * _dot(q, c_prev.astype(BF16)) + _dot(s_qk.astype(BF16), v)
        q_n = jnp.sum(q.astype(F32) * n_prev, axis=-1, keepdims=True)
        den = w_inter * q_n + jnp.sum(s_qk, axis=-1, keepdims=True)
        hid = num / jnp.maximum(jnp.abs(den), jnp.exp(-m_t))

        b_last = b_col[chunk - 1:chunk, :]
        w_s = b_last - b_col + i_col
        m_new = jnp.maximum(b_last + m_prev, jnp.max(w_s, axis=0, keepdims=True))
        decay = jnp.exp(b_last + m_prev - m_new)
        kw = jnp.exp(w_s - m_new) * ks
        c_scr[h] = decay * c_prev + _dot_tn(kw.astype(BF16), v)
        n_scr[h] = decay * n_prev + jnp.sum(kw, axis=0, keepdims=True)
        m_scr[h] = m_new

        o_ref[:, cols] = (hid * jax.nn.sigmoid(og_ref[:, cols].astype(F32))).astype(o_ref.dtype)


def _mlstm(proj3, gates3, *, col0, nh, chunk=MLSTM_CHUNK):
    bsz, s, _ = proj3.shape
    dh = 256
    w = nh * dh
    cb = col0 // w
    blk = lambda k: pl.BlockSpec((None, chunk, w), lambda b, i: (b, i, cb + k))
    return pl.pallas_call(
        functools.partial(_mlstm_kernel, chunk=chunk, nh=nh, dh=dh),
        grid=(bsz, s // chunk),
        in_specs=[blk(0), blk(1), blk(2), blk(3),
                  pl.BlockSpec((None, chunk, GATE_LANES), lambda b, i: (b, i, 0))],
        out_specs=pl.BlockSpec((None, chunk, w), lambda b, i: (b, i, 0)),
        out_shape=jax.ShapeDtypeStruct((bsz, s, w), BF16),
        scratch_shapes=[pltpu.VMEM((nh, dh, dh), F32), pltpu.VMEM((nh, 1, dh), F32),
                        pltpu.VMEM((nh, 1, 1), F32)],
        compiler_params=_cparams("parallel", "arbitrary"),
        name="mlstm_chunkwise",
    )(proj3, proj3, proj3, proj3, gates3)


def kernel(x, c, mod_w, mod_b, norm_mix_g, norm_ffn_g, ffn_w1, ffn_w3, ffn_w2, ab_w_in, ab_conv_w, ab_w_out, cd_w_in, cd_gate_b, rg_conv_w, rg_conv_b, rg_wa, rg_ba, rg_wx, rg_bx, rg_lambda, cd_w_out, final_norm_g):
    bsz, seq, d = x.shape
    depth = mod_w.shape[0]
    t = bsz * seq
    half = d // 2
    a_heads = half // A_HEAD_DIM
    slopes = 2.0 ** (-8.0 * jnp.arange(1, a_heads + 1, dtype=F32) / a_heads)

    mod = _modulation(c, mod_w, mod_b)
    x2 = x.reshape(t, d)
    for layer in range(depth):
        mod_l = mod[layer]
        if layer % 2 == 0:
            e = layer // 2
            proj = _in_proj(x2, norm_mix_g[layer], mod_l, ab_w_in[e].astype(BF16), seq,
                            shift_row=0, scale_row=1)
            proj3 = proj.reshape(bsz, seq, proj.shape[1])
            y_a = _moba(proj3, slopes, a_heads).reshape(t, half)
            y_b = _gated_conv(proj, ab_conv_w[e], seq, col0=3 * half)
            w_out = ab_w_out[e]
        else:
            o = layer // 2
            w_in = cd_w_in[o]
            n_main = 2 * half + 4 * half
            n_g = w_in.shape[1] - n_main
            w_gate = jnp.zeros((d, GATE_LANES), BF16).at[:, :n_g].set(w_in[:, n_main:].astype(BF16))
            b_gate = jnp.zeros((1, GATE_LANES), F32).at[0, :n_g].set(cd_gate_b[o])
            proj, gates = _in_proj(x2, norm_mix_g[layer], mod_l, w_in[:, :n_main].astype(BF16), seq,
                                   shift_row=0, scale_row=1, w_gate=w_gate, b_gate=b_gate)
            proj3 = proj.reshape(bsz, seq, n_main)
            w_cat = jnp.concatenate([rg_wa[o], rg_wx[o]], axis=-1).astype(BF16)
            y_a = _rglru(proj3, rg_conv_w[o], rg_conv_b[o], w_cat, rg_ba[o], rg_bx[o],
                         rg_lambda[o]).reshape(t, half)
            y_b = _mlstm(proj3, gates.reshape(bsz, seq, GATE_LANES), col0=2 * half,
                         nh=D_HEADS).reshape(t, half)
            w_out = cd_w_out[o]
        x2 = _out_proj(y_a, y_b, w_out.astype(BF16), x2, mod_l, seq, gate_row=2)
        final_g = final_norm_g if layer == depth - 1 else None
        x2 = _ffn(x2, norm_ffn_g[layer], mod_l, ffn_w1[layer].astype(BF16), ffn_w3[layer].astype(BF16),
                  ffn_w2[layer].astype(BF16), seq, final_g)
    return x2.reshape(bsz, seq, d)
```

```python
import functools

import jax
import jax.numpy as jnp
from jax import lax
from jax.experimental import pallas as pl
from jax.experimental.pallas import tpu as pltpu

F32 = jnp.float32
BF16 = jnp.bfloat16

EPS = 1e-6
NEG_INF = -1e30
LOG2E = 1.4426950408889634

A_HEAD_DIM = 128
MOBA_BLOCK = 256
MOBA_TOPK = 3
MOBA_GROUP = 4
RANK_QBLOCKS = 8
B_CONV = 3
C_BLOCKS = 8
C_CONV = 4
RG_C = 8.0
D_HEADS = 4
MLSTM_CHUNK = 256
GATE_LANES = 128

VMEM_LIMIT_BYTES = 60 * 1024 * 1024
HALO_ROWS = 8
NORM_CHUNKS = 4
FFN_ROW_CHUNKS = 2


def _cparams(*sem):
    return pltpu.CompilerParams(dimension_semantics=sem, vmem_limit_bytes=VMEM_LIMIT_BYTES)


def _dot(a, b):
    return jnp.dot(a, b, preferred_element_type=F32)


def _dot_nt(a, b):
    return lax.dot_general(a, b, (((1,), (1,)), ((), ())), preferred_element_type=F32)


def _dot_tn(a, b):
    return lax.dot_general(a, b, (((0,), (0,)), ((), ())), preferred_element_type=F32)


def _norm_mod(x, g, scale, shift):
    xn = x * lax.rsqrt(jnp.mean(x * x, axis=-1, keepdims=True) + EPS)
    return (xn * g) * (1.0 + scale) + shift


def _mod_kernel(c_ref, w_ref, b_ref, o_ref):
    c = c_ref[...]
    cond = c * jax.nn.sigmoid(c)
    o_ref[...] = _dot(cond.astype(BF16), w_ref[...].astype(BF16)) + b_ref[...]


def _modulation(c, mod_w, mod_b):
    depth, d, n = mod_w.shape
    bsz = c.shape[0]
    rows = 8
    c8 = jnp.zeros((rows, d), F32).at[:bsz].set(c)
    tn = 1024
    out = pl.pallas_call(
        _mod_kernel,
        grid=(depth, n // tn),
        in_specs=[
            pl.BlockSpec((rows, d), lambda l, j: (0, 0)),
            pl.BlockSpec((None, d, tn), lambda l, j: (l, 0, j)),
            pl.BlockSpec((None, 1, tn), lambda l, j: (l, 0, j)),
        ],
        out_specs=pl.BlockSpec((None, rows, tn), lambda l, j: (l, 0, j)),
        out_shape=jax.ShapeDtypeStruct((depth, rows, n), F32),
        compiler_params=_cparams("arbitrary", "arbitrary"),
        name="adaln_modulation",
    )(c8, mod_w, mod_b.reshape(depth, 1, n))
    return out[:, :bsz].reshape(depth, bsz, 6, d)


def _in_proj_kernel(x_ref, g_ref, mod_ref, w_ref, *rest, shift_row, scale_row, with_gates):
    if with_gates:
        wg_ref, bg_ref, o_ref, og_ref, h_scr = rest
    else:
        o_ref, h_scr = rest
    j = pl.program_id(1)

    @pl.when(j == 0)
    def _():
        rc = x_ref.shape[0] // NORM_CHUNKS
        for r in range(NORM_CHUNKS):
            rows = slice(r * rc, (r + 1) * rc)
            h = _norm_mod(x_ref[rows, :], g_ref[...], mod_ref[scale_row:scale_row + 1, :],
                          mod_ref[shift_row:shift_row + 1, :])
            hb = h.astype(BF16)
            h_scr[rows, :] = hb
            o_ref[rows, :] = _dot(hb, w_ref[...]).astype(o_ref.dtype)
            if with_gates:
                og_ref[rows, :] = _dot(hb, wg_ref[...]) + bg_ref[...]

    @pl.when(j > 0)
    def _():
        o_ref[...] = _dot(h_scr[...], w_ref[...]).astype(o_ref.dtype)


def _in_proj(x2, g, mod_l, w, idx, n, seq, *, shift_row, scale_row, w_gate=None, b_gate=None, tm=1024, tn=1024):
    t, d = x2.shape
    tiles_per_seq = seq // tm
    with_gates = w_gate is not None
    in_specs = [
        pl.BlockSpec((tm, d), lambda i, j: (i, 0)),
        pl.BlockSpec((1, d), lambda i, j: (0, 0)),
        pl.BlockSpec((None, 6, d), lambda i, j: (i // tiles_per_seq, 0, 0)),
        pl.BlockSpec((None, d, tn), lambda i, j: (idx, 0, j)),
    ]
    args = [x2, g.reshape(1, d), mod_l, w]
    out_specs = pl.BlockSpec((tm, tn), lambda i, j: (i, j))
    out_shape = jax.ShapeDtypeStruct((t, n), BF16)
    if with_gates:
        in_specs += [pl.BlockSpec((d, GATE_LANES), lambda i, j: (0, 0)),
                     pl.BlockSpec((1, GATE_LANES), lambda i, j: (0, 0))]
        args += [w_gate, b_gate]
        out_specs = [out_specs, pl.BlockSpec((tm, GATE_LANES), lambda i, j: (i, 0))]
        out_shape = [out_shape, jax.ShapeDtypeStruct((t, GATE_LANES), F32)]
    return pl.pallas_call(
        functools.partial(_in_proj_kernel, shift_row=shift_row, scale_row=scale_row, with_gates=with_gates),
        grid=(t // tm, n // tn),
        in_specs=in_specs,
        out_specs=out_specs,
        out_shape=out_shape,
        scratch_shapes=[pltpu.VMEM((tm, d), BF16)],
        compiler_params=_cparams("parallel", "arbitrary"),
        name="norm_in_proj_gates" if with_gates else "norm_in_proj",
    )(*args)


def _out_proj_kernel(a_ref, b_ref, w_ref, x_ref, mod_ref, o_ref, *, gate_row, half):
    acc = _dot(a_ref[...], w_ref[0:half, :]) + _dot(b_ref[...], w_ref[half:, :])
    o_ref[...] = x_ref[...] + mod_ref[gate_row:gate_row + 1, :] * acc


def _out_proj(a, b, w, idx, x2, mod_l, seq, *, gate_row, tm=512):
    t, d = x2.shape
    half = a.shape[1]
    tiles_per_seq = seq // tm
    return pl.pallas_call(
        functools.partial(_out_proj_kernel, gate_row=gate_row, half=half),
        grid=(t // tm,),
        in_specs=[
            pl.BlockSpec((tm, half), lambda i: (i, 0)),
            pl.BlockSpec((tm, half), lambda i: (i, 0)),
            pl.BlockSpec((None, 2 * half, d), lambda i: (idx, 0, 0)),
            pl.BlockSpec((tm, d), lambda i: (i, 0)),
            pl.BlockSpec((None, 6, d), lambda i: (i // tiles_per_seq, 0, 0)),
        ],
        out_specs=pl.BlockSpec((tm, d), lambda i: (i, 0)),
        out_shape=jax.ShapeDtypeStruct((t, d), F32),
        compiler_params=_cparams("parallel"),
        name="out_proj_residual",
    )(a, b, w, x2, mod_l)


def _ffn_kernel(x_ref, g_ref, mod_ref, w1_ref, w3_ref, w2_ref, *rest, final):
    if final:
        fg_ref, o_ref, h_scr = rest
    else:
        o_ref, h_scr = rest
    j = pl.program_id(1)
    tm = x_ref.shape[0]

    def hidden_tile(h):
        a = _dot(h, w1_ref[...])
        b = _dot(h, w3_ref[...])
        gated = (a * jax.nn.sigmoid(a)) * b
        return _dot(gated.astype(BF16), w2_ref[...])

    def row_chunks(n):
        rc = tm // n
        return [slice(r * rc, (r + 1) * rc) for r in range(n)]

    @pl.when(j == 0)
    def _():
        for rows in row_chunks(NORM_CHUNKS):
            h = _norm_mod(x_ref[rows, :], g_ref[...], mod_ref[4:5, :], mod_ref[3:4, :]).astype(BF16)
            h_scr[rows, :] = h
            o_ref[rows, :] = hidden_tile(h)

    @pl.when(j > 0)
    def _():
        for rows in row_chunks(FFN_ROW_CHUNKS):
            o_ref[rows, :] += hidden_tile(h_scr[rows, :])

    @pl.when(j == pl.num_programs(1) - 1)
    def _():
        for rows in row_chunks(FFN_ROW_CHUNKS):
            y = x_ref[rows, :] + mod_ref[5:6, :] * o_ref[rows, :]
            if final:
                y = y * lax.rsqrt(jnp.mean(y * y, axis=-1, keepdims=True) + EPS) * fg_ref[...]
            o_ref[rows, :] = y


def _ffn(x2, g, mod_l, w1, w3, w2, layer, seq, final_g=None, *, tm=1024, tf=512):
    t, d = x2.shape
    f = w1.shape[2]
    tiles_per_seq = seq // tm
    final = final_g is not None
    in_specs = [
        pl.BlockSpec((tm, d), lambda i, j: (i, 0)),
        pl.BlockSpec((1, d), lambda i, j: (0, 0)),
        pl.BlockSpec((None, 6, d), lambda i, j: (i // tiles_per_seq, 0, 0)),
        pl.BlockSpec((None, d, tf), lambda i, j: (layer, 0, j)),
        pl.BlockSpec((None, d, tf), lambda i, j: (layer, 0, j)),
        pl.BlockSpec((None, tf, d), lambda i, j: (layer, j, 0)),
    ]
    args = [x2, g.reshape(1, d), mod_l, w1, w3, w2]
    if final:
        in_specs.append(pl.BlockSpec((1, d), lambda i, j: (0, 0)))
        args.append(final_g.reshape(1, d))
    return pl.pallas_call(
        functools.partial(_ffn_kernel, final=final),
        grid=(t // tm, f // tf),
        in_specs=in_specs,
        out_specs=pl.BlockSpec((tm, d), lambda i, j: (i, 0)),
        out_shape=jax.ShapeDtypeStruct((t, d), F32),
        scratch_shapes=[pltpu.VMEM((tm, d), BF16)],
        compiler_params=_cparams("parallel", "arbitrary"),
        name="swiglu_ffn_final" if final else "swiglu_ffn",
    )(*args)


def _gated_conv_kernel(gb_ref, gc_ref, xb_ref, gch_ref, xbh_ref, w_ref, o_ref, u_scr, *, tiles_per_seq, tm):
    i = pl.program_id(0)
    u = gc_ref[...].astype(F32) * xb_ref[...].astype(F32)
    uh = gch_ref[...].astype(F32) * xbh_ref[...].astype(F32)
    first = (i % tiles_per_seq) == 0
    u_scr[0:HALO_ROWS, :] = jnp.where(first, 0.0, uh)
    u_scr[HALO_ROWS:, :] = u
    w = w_ref[...]
    conv = w[2:3, :] * u
    conv += w[1:2, :] * u_scr[HALO_ROWS - 1:HALO_ROWS - 1 + tm, :]
    conv += w[0:1, :] * u_scr[HALO_ROWS - 2:HALO_ROWS - 2 + tm, :]
    o_ref[...] = (gb_ref[...].astype(F32) * conv).astype(o_ref.dtype)


def _gated_conv(proj, conv_w, seq, *, col0, tm=1024):
    t = proj.shape[0]
    cw = conv_w.shape[1]
    cb = col0 // cw
    tiles_per_seq = seq // tm
    hb = tm // HALO_ROWS

    def halo_map(k):
        return lambda i: (jnp.maximum(i * hb - 1, 0), cb + k)

    return pl.pallas_call(
        functools.partial(_gated_conv_kernel, tiles_per_seq=tiles_per_seq, tm=tm),
        grid=(t // tm,),
        in_specs=[
            pl.BlockSpec((tm, cw), lambda i: (i, cb)),
            pl.BlockSpec((tm, cw), lambda i: (i, cb + 1)),
            pl.BlockSpec((tm, cw), lambda i: (i, cb + 2)),
            pl.BlockSpec((HALO_ROWS, cw), halo_map(1)),
            pl.BlockSpec((HALO_ROWS, cw), halo_map(2)),
            pl.BlockSpec((B_CONV, cw), lambda i: (0, 0)),
        ],
        out_specs=pl.BlockSpec((tm, cw), lambda i: (i, 0)),
        out_shape=jax.ShapeDtypeStruct((t, cw), BF16),
        scratch_shapes=[pltpu.VMEM((tm + HALO_ROWS, cw), F32)],
        compiler_params=_cparams("parallel"),
        name="gated_short_conv",
    )(proj, proj, proj, proj, proj, conv_w)


def _split3(x):
    hi = x.astype(BF16).astype(F32)
    mid = (x - hi).astype(BF16).astype(F32)
    lo = (x - hi - mid).astype(BF16).astype(F32)
    return hi, mid, lo


def _moba_prep_kernel(slopes_ref, q_ref, k_ref, v_ref, qx_ref, kx_ref, vt_ref, *, nb, blk, grp):
    s, dh = q_ref.shape
    u = slopes_ref[pl.program_id(1)] * (dh ** 0.5)
    kmean = jnp.mean(k_ref[...].astype(F32).reshape(nb, blk, dh), axis=1)
    hi = kmean.astype(BF16)
    lo = (kmean - hi.astype(F32)).astype(BF16)
    q = q_ref[...]
    gate = _dot_nt(hi, q) + _dot_nt(lo, q)
    lanes_per = RANK_QBLOCKS * blk
    pieces = []
    for c in range(s // lanes_per):
        nr = min(nb, RANK_QBLOCKS * (c + 1))
        n_idx = lax.broadcasted_iota(jnp.int32, (nr, lanes_per), 0)
        q_blk = lax.broadcasted_iota(jnp.int32, (nr, lanes_per), 1) // blk + c * RANK_QBLOCKS
        eligible = n_idx < q_blk
        g = jnp.where(eligible, gate[0:nr, c * lanes_per:(c + 1) * lanes_per], NEG_INF)
        rank = jnp.zeros((nr, lanes_per), jnp.int32)
        for m in range(nr):
            gm = g[m:m + 1, :]
            beats = (gm > g) | ((gm == g) & (n_idx > m))
            rank += beats.astype(jnp.int32)
        keep = (eligible & (rank < MOBA_TOPK)) | (n_idx == q_blk)
        piece = jnp.where(keep, 0.0, NEG_INF)
        if nr < nb:
            piece = jnp.concatenate([piece, jnp.full((nb - nr, lanes_per), NEG_INF, F32)], axis=0)
        pieces.append(piece)
    sel_bias = jnp.concatenate(pieces, axis=1) if len(pieces) > 1 else pieces[0]

    off = (lax.broadcasted_iota(jnp.int32, (8, s), 1) % blk).astype(F32) * u
    hi, mid, lo = _split3(off)
    feat = lax.broadcasted_iota(jnp.int32, (8, s), 0)
    pos_rows = jnp.where(feat == 0, hi, jnp.where(feat == 1, mid, jnp.where(feat == 2, lo,
                         jnp.where(feat < 6, 1.0, 0.0))))
    ext_t = jnp.concatenate([sel_bias, pos_rows, jnp.zeros((dh - nb - 8, s), F32)], axis=0)
    chunk = grp * blk
    for ci in range(s // chunk):
        rows = slice(ci * chunk, (ci + 1) * chunk)
        qx_ref[rows, 0:dh] = q_ref[rows, :]
        qx_ref[rows, dh:2 * dh] = ext_t[:, rows].T.astype(qx_ref.dtype)

    col = lax.broadcasted_iota(jnp.int32, (blk, dh), 1)
    khi, kmid, klo = _split3(lax.broadcasted_iota(jnp.int32, (blk, dh), 0).astype(F32) * u)
    tile = jnp.where((col >= nb) & (col < nb + 3), -1.0,
                     jnp.where(col == nb + 3, khi, jnp.where(col == nb + 4, kmid,
                               jnp.where(col == nb + 5, klo, 0.0))))
    kx_ref[:, 0:dh] = k_ref[...]
    for n in range(nb):
        kx_ref[n * blk:(n + 1) * blk, dh:2 * dh] = jnp.where(col == n, 1.0, tile).astype(kx_ref.dtype)

    for n in range(nb):
        v_t = v_ref[n * blk:(n + 1) * blk, :].astype(F32).T.astype(vt_ref.dtype)
        vt_ref[n // grp, :, (n % grp) * blk:(n % grp + 1) * blk] = v_t


def _moba_attn_kernel(slopes_ref, qx_ref, kx_ref, vt_ref, o_ref, m_scr, l_scr, acc_scr, *, blk, grp, scale):
    h = pl.program_id(1)
    qt = pl.program_id(2)
    slope2 = slopes_ref[h] * LOG2E
    c2 = scale * LOG2E
    causal = (lax.broadcasted_iota(jnp.int32, (blk, blk), 0)
              <= lax.broadcasted_iota(jnp.int32, (blk, blk), 1))

    def scores(b, kg):
        return _dot_nt(kg, qx_ref[b * blk:(b + 1) * blk, :])

    def update(b, raw, vt_g, n0, n_sub, first):
        q_blk = qt * grp + b
        subs, shifts = [], []
        m_new = None if first else m_scr[b]
        for gi in range(n_sub):
            r = raw[gi * blk:(gi + 1) * blk, :]
            if first and gi == b:
                r = jnp.where(causal, r, NEG_INF)
            shift = -slope2 * (blk * (q_blk - (n0 + gi))).astype(F32)
            cm = jnp.max(r, axis=0, keepdims=True) * c2 + shift
            m_new = cm if m_new is None else jnp.maximum(m_new, cm)
            subs.append(r)
            shifts.append(shift)
        l = None
        probs = []
        for r, shift in zip(subs, shifts):
            p = jnp.exp2(r * c2 - (m_new - shift))
            ps = jnp.sum(p, axis=0, keepdims=True)
            l = ps if l is None else l + ps
            probs.append(p.astype(BF16))
        pv = _dot(vt_g, jnp.concatenate(probs, axis=0) if n_sub > 1 else probs[0])
        if first:
            acc_scr[b] = pv
        else:
            alpha = jnp.exp2(m_scr[b] - m_new)
            l = alpha * l_scr[b] + l
            acc_scr[b] = alpha * acc_scr[b] + pv
        m_scr[b] = m_new
        l_scr[b] = l


    base = pl.multiple_of(qt * (grp * blk), grp * blk)
    raws = [scores(b, kx_ref[pl.ds(base, (b + 1) * blk), :]) for b in range(grp)]
    for b in range(grp):
        update(b, raws[b], vt_ref[qt, :, 0:(b + 1) * blk], qt * grp, b + 1, True)

    def body(g, carry):
        kg = kx_ref[pl.ds(pl.multiple_of(g * (grp * blk), grp * blk), grp * blk), :]
        vt_g = vt_ref[g]
        raws = [scores(b, kg) for b in range(grp)]
        for b in range(grp):
            update(b, raws[b], vt_g, g * grp, grp, False)
        return carry

    lax.fori_loop(0, qt, body, 0)
    for b in range(grp):
        o_ref[b * blk:(b + 1) * blk, :] = (acc_scr[b] / l_scr[b]).T.astype(o_ref.dtype)


def _moba(proj3, slopes, n_heads):
    bsz, s, _ = proj3.shape
    dh, blk, grp = A_HEAD_DIM, MOBA_BLOCK, MOBA_GROUP
    nb = s // blk
    ng = nb // grp
    tq = grp * blk
    smem = pl.BlockSpec(memory_space=pltpu.SMEM)
    qx, kx, vt = pl.pallas_call(
        functools.partial(_moba_prep_kernel, nb=nb, blk=blk, grp=grp),
        grid=(bsz, n_heads),
        in_specs=[
            smem,
            pl.BlockSpec((None, s, dh), lambda b, h: (b, 0, h)),
            pl.BlockSpec((None, s, dh), lambda b, h: (b, 0, n_heads + h)),
            pl.BlockSpec((None, s, dh), lambda b, h: (b, 0, 2 * n_heads + h)),
        ],
        out_specs=[
            pl.BlockSpec((None, None, s, 2 * dh), lambda b, h: (b, h, 0, 0)),
            pl.BlockSpec((None, None, s, 2 * dh), lambda b, h: (b, h, 0, 0)),
            pl.BlockSpec((None, None, ng, dh, tq), lambda b, h: (b, h, 0, 0, 0)),
        ],
        out_shape=[
            jax.ShapeDtypeStruct((bsz, n_heads, s, 2 * dh), BF16),
            jax.ShapeDtypeStruct((bsz, n_heads, s, 2 * dh), BF16),
            jax.ShapeDtypeStruct((bsz, n_heads, ng, dh, tq), BF16),
        ],
        compiler_params=_cparams("parallel", "parallel"),
        name="moba_select",
    )(slopes, proj3, proj3, proj3)
    return pl.pallas_call(
        functools.partial(_moba_attn_kernel, blk=blk, grp=grp, scale=dh ** -0.5),
        grid=(bsz, n_heads, ng),
        in_specs=[
            smem,
            pl.BlockSpec((None, None, tq, 2 * dh), lambda b, h, i: (b, h, i, 0)),
            pl.BlockSpec((None, None, s, 2 * dh), lambda b, h, i: (b, h, 0, 0)),
            pl.BlockSpec((None, None, ng, dh, tq), lambda b, h, i: (b, h, 0, 0, 0)),
        ],
        out_specs=pl.BlockSpec((None, tq, dh), lambda b, h, i: (b, i, h)),
        out_shape=jax.ShapeDtypeStruct((bsz, s, n_heads * dh), BF16),
        scratch_shapes=[pltpu.VMEM((grp, 1, blk), F32), pltpu.VMEM((grp, 1, blk), F32),
                        pltpu.VMEM((grp, dh, blk), F32)],
        compiler_params=_cparams("parallel", "parallel", "arbitrary"),
        name="moba_attention",
    )(slopes, qx, kx, vt)


def _shift_rows(x, d, fill):
    n, c = x.shape
    if d % HALO_ROWS == 0:
        return jnp.concatenate([jnp.full((d, c), fill, x.dtype), x[:n - d]], axis=0)
    rolled = pltpu.roll(x, d, axis=0)
    rows = lax.broadcasted_iota(jnp.int32, x.shape, 0)
    return jnp.where(rows < d, fill, rolled)


def _rglru_kernel(x_ref, xh_ref, gr_ref, cw_ref, cb_ref, w_ref, ba_ref, bx_ref, lam_ref, o_ref,
                  h_scr, *, ts, n_blocks):
    si = pl.program_id(1)

    @pl.when(si == 0)
    def _():
        h_scr[...] = jnp.zeros_like(h_scr)

    xb = x_ref[...]
    cw = cw_ref[...]
    halo = jnp.where(si == 0, 0.0, xh_ref[...].astype(F32))
    halo_row = lax.broadcasted_iota(jnp.int32, halo.shape, 0)
    delta = (lax.broadcasted_iota(jnp.int32, (ts, ts), 0) - lax.broadcasted_iota(jnp.int32, (ts, ts), 1))
    xc = cw[C_CONV - 1:C_CONV, :] * xb.astype(F32) + cb_ref[...]
    head = jnp.zeros_like(halo)
    for j in range(C_CONV - 1):
        d = C_CONV - 1 - j
        shifted = _dot(jnp.where(delta == d, 1.0, 0.0).astype(BF16), xb)
        xc += cw[j:j + 1, :] * shifted
        head += cw[j:j + 1, :] * jnp.where(halo_row < d, pltpu.roll(halo, d, axis=0), 0.0)
    xc = jnp.concatenate([xc[:HALO_ROWS] + head, xc[HALO_ROWS:]], axis=0)

    bd = xc.shape[1] // n_blocks
    xcb = xc.astype(BF16)
    gates = [_dot(xcb[:, g * bd:(g + 1) * bd], w_ref[g]) for g in range(n_blocks)]
    r = jax.nn.sigmoid(jnp.concatenate([gt[:, :bd] for gt in gates], axis=1) + ba_ref[...])
    i = jax.nn.sigmoid(jnp.concatenate([gt[:, bd:] for gt in gates], axis=1) + bx_ref[...])
    log_a = (-RG_C) * r * jax.nn.softplus(-lam_ref[...])
    a = jnp.exp(log_a)
    z = -jnp.tanh(log_a) * (a * a + 1.0)
    u = jnp.where(z > 0.0, z * lax.rsqrt(z), 0.0) * (i * xc)

    d = 1
    while d < ts:
        u = a * _shift_rows(u, d, 0.0) + u
        a = a * _shift_rows(a, d, 1.0)
        d *= 2
    hs = a * h_scr[...] + u
    h_scr[...] = hs[ts - 1:ts, :]
    o_ref[...] = (hs * jax.nn.gelu(gr_ref[...].astype(F32))).astype(o_ref.dtype)


def _rglru(proj3, conv_w, conv_b, w_cat, ba, bx, lam, *, ts=256):
    bsz, s, _ = proj3.shape
    c = conv_w.shape[1]
    n_blocks, bd, _ = w_cat.shape
    hb = ts // HALO_ROWS
    vec = pl.BlockSpec((1, c), lambda b, i: (0, 0))
    return pl.pallas_call(
        functools.partial(_rglru_kernel, ts=ts, n_blocks=n_blocks),
        grid=(bsz, s // ts),
        in_specs=[
            pl.BlockSpec((None, ts, c), lambda b, i: (b, i, 0)),
            pl.BlockSpec((None, HALO_ROWS, c), lambda b, i: (b, jnp.maximum(i * hb - 1, 0), 0)),
            pl.BlockSpec((None, ts, c), lambda b, i: (b, i, 1)),
            pl.BlockSpec((C_CONV, c), lambda b, i: (0, 0)),
            vec,
            pl.BlockSpec((n_blocks, bd, 2 * bd), lambda b, i: (0, 0, 0)),
            vec, vec, vec,
        ],
        out_specs=pl.BlockSpec((None, ts, c), lambda b, i: (b, i, 0)),
        out_shape=jax.ShapeDtypeStruct((bsz, s, c), BF16),
        scratch_shapes=[pltpu.VMEM((1, c), F32)],
        compiler_params=_cparams("parallel", "arbitrary"),
        name="rglru_scan",
    )(proj3, proj3, proj3, conv_w, conv_b.reshape(1, c), w_cat, ba.reshape(1, c), bx.reshape(1, c),
      lam.reshape(1, c))


def _mlstm_kernel(q_ref, k_ref, v_ref, og_ref, gt_ref, o_ref, c_scr, n_scr, m_scr, *, chunk, nh, dh):
    ci = pl.program_id(1)

    @pl.when(ci == 0)
    def _():
        c_scr[...] = jnp.zeros_like(c_scr)
        n_scr[...] = jnp.zeros_like(n_scr)
        m_scr[...] = jnp.zeros_like(m_scr)

    gates = gt_ref[...]
    bcum = jax.nn.log_sigmoid(gates)
    d = 1
    while d < chunk:
        bcum = bcum + _shift_rows(bcum, d, 0.0)
        d *= 2
    lane = lax.broadcasted_iota(jnp.int32, gates.shape, 1)
    rows_t = jnp.where(lane < nh, gates, bcum).T
    r_idx = lax.broadcasted_iota(jnp.int32, (chunk, chunk), 0)
    c_idx = lax.broadcasted_iota(jnp.int32, (chunk, chunk), 1)
    causal = c_idx <= r_idx
    k_scale = dh ** -0.5

    for h in range(nh):
        cols = slice(h * dh, (h + 1) * dh)
        i_col = gates[:, h:h + 1]
        b_col = bcum[:, nh + h:nh + h + 1]
        i_row = rows_t[h:h + 1, :]
        b_row = rows_t[nh + h:nh + h + 1, :]
        m_prev = m_scr[h]

        dmat = jnp.where(causal, b_col - b_row + i_row, NEG_INF)
        inter = b_col + m_prev
        m_t = jnp.maximum(inter, jnp.max(dmat, axis=-1, keepdims=True))
        w_intra = jnp.exp(dmat - m_t)
        w_inter = jnp.exp(inter - m_t)

        q = q_ref[:, cols]
        ks = k_ref[:, cols].astype(F32) * k_scale
        ksb = ks.astype(BF16)
        v = v_ref[:, cols]
        s_qk = _dot_nt(q, ksb) * w_intra
        c_prev = c_scr[h]
        n_prev = n_scr[h]
        num = w_inter * _dot(q, c_prev.astype(BF16)) + _dot(s_qk.astype(BF16), v)
        q_n = jnp.sum(q.astype(F32) * n_prev, axis=-1, keepdims=True)
        den = w_inter * q_n + jnp.sum(s_qk, axis=-1, keepdims=True)
        hid = num / jnp.maximum(jnp.abs(den), jnp.exp(-m_t))

        b_last = b_col[chunk - 1:chunk, :]
        w_s = b_last - b_col + i_col
        m_new = jnp.maximum(b_last + m_prev, jnp.max(w_s, axis=0, keepdims=True))
        decay = jnp.exp(b_last + m_prev - m_new)
        kw = jnp.exp(w_s - m_new) * ks
        c_scr[h] = decay * c_prev + _dot_tn(kw.astype(BF16), v)
        n_scr[h] = decay * n_prev + jnp.sum(kw, axis=0, keepdims=True)
        m_scr[h] = m_new

        o_ref[:, cols] = (hid * jax.nn.sigmoid(og_ref[:, cols].astype(F32))).astype(o_ref.dtype)


def _mlstm(proj3, gates3, *, col0, nh, chunk=MLSTM_CHUNK):
    bsz, s, _ = proj3.shape
    dh = 256
    w = nh * dh
    cb = col0 // w
    blk = lambda k: pl.BlockSpec((None, chunk, w), lambda b, i: (b, i, cb + k))
    return pl.pallas_call(
        functools.partial(_mlstm_kernel, chunk=chunk, nh=nh, dh=dh),
        grid=(bsz, s // chunk),
        in_specs=[blk(0), blk(1), blk(2), blk(3),
                  pl.BlockSpec((None, chunk, GATE_LANES), lambda b, i: (b, i, 0))],
        out_specs=pl.BlockSpec((None, chunk, w), lambda b, i: (b, i, 0)),
        out_shape=jax.ShapeDtypeStruct((bsz, s, w), BF16),
        scratch_shapes=[pltpu.VMEM((nh, dh, dh), F32), pltpu.VMEM((nh, 1, dh), F32),
                        pltpu.VMEM((nh, 1, 1), F32)],
        compiler_params=_cparams("parallel", "arbitrary"),
        name="mlstm_chunkwise",
    )(proj3, proj3, proj3, proj3, gates3)


def kernel(x, c, mod_w, mod_b, norm_mix_g, norm_ffn_g, ffn_w1, ffn_w3, ffn_w2, ab_w_in, ab_conv_w, ab_w_out, cd_w_in, cd_gate_b, rg_conv_w, rg_conv_b, rg_wa, rg_ba, rg_wx, rg_bx, rg_lambda, cd_w_out, final_norm_g):
    bsz, seq, d = x.shape
    depth = mod_w.shape[0]
    t = bsz * seq
    half = d // 2
    a_heads = half // A_HEAD_DIM
    slopes = 2.0 ** (-8.0 * jnp.arange(1, a_heads + 1, dtype=F32) / a_heads)

    mod = _modulation(c, mod_w, mod_b)
    ffn_w1_b, ffn_w3_b, ffn_w2_b = ffn_w1.astype(BF16), ffn_w3.astype(BF16), ffn_w2.astype(BF16)
    ab_w_in_b, ab_w_out_b = ab_w_in.astype(BF16), ab_w_out.astype(BF16)
    cd_w_in_b, cd_w_out_b = cd_w_in.astype(BF16), cd_w_out.astype(BF16)
    x2 = x.reshape(t, d)
    for layer in range(depth):
        mod_l = mod[layer]
        if layer % 2 == 0:
            e = layer // 2
            n_main = 6 * half
            proj = _in_proj(x2, norm_mix_g[layer], mod_l, ab_w_in_b, e, n_main, seq, shift_row=0, scale_row=1)
            proj3 = proj.reshape(bsz, seq, n_main)
            y_a = _moba(proj3, slopes, a_heads).reshape(t, half)
            y_b = _gated_conv(proj, ab_conv_w[e], seq, col0=3 * half)
            w_out, w_idx = ab_w_out_b, e
        else:
            o = layer // 2
            n_main = 2 * half + 4 * half
            n_g = cd_w_in.shape[2] - n_main
            w_gate = jnp.zeros((d, GATE_LANES), BF16).at[:, :n_g].set(cd_w_in[o, :, n_main:].astype(BF16))
            b_gate = jnp.zeros((1, GATE_LANES), F32).at[0, :n_g].set(cd_gate_b[o])
            proj, gates = _in_proj(x2, norm_mix_g[layer], mod_l, cd_w_in_b, o, n_main, seq,
                                   shift_row=0, scale_row=1, w_gate=w_gate, b_gate=b_gate)
            proj3 = proj.reshape(bsz, seq, n_main)
            w_cat = jnp.concatenate([rg_wa[o], rg_wx[o]], axis=-1).astype(BF16)
            y_a = _rglru(proj3, rg_conv_w[o], rg_conv_b[o], w_cat, rg_ba[o], rg_bx[o],
                         rg_lambda[o]).reshape(t, half)
            y_b = _mlstm(proj3, gates.reshape(bsz, seq, GATE_LANES), col0=2 * half,
                         nh=D_HEADS).reshape(t, half)
            w_out, w_idx = cd_w_out_b, o
        x2 = _out_proj(y_a, y_b, w_out, w_idx, x2, mod_l, seq, gate_row=2)
        final_g = final_norm_g if layer == depth - 1 else None
        x2 = _ffn(x2, norm_ffn_g[layer], mod_l, ffn_w1_b, ffn_w3_b, ffn_w2_b, layer, seq, final_g)
    return x2.reshape(bsz, seq, d)
```

```python
import functools

import jax
import jax.numpy as jnp
from jax import lax
from jax.experimental import pallas as pl
from jax.experimental.pallas import tpu as pltpu

F32 = jnp.float32
BF16 = jnp.bfloat16
F8 = jnp.float8_e4m3fn
F8_MAX = 448.0
F8_TINY = 1e-30

EPS = 1e-6
NEG_INF = -1e30
LOG2E = 1.4426950408889634

A_HEAD_DIM = 128
MOBA_BLOCK = 256
MOBA_TOPK = 3
MOBA_GROUP = 4
RANK_QBLOCKS = 8
B_CONV = 3
C_BLOCKS = 8
C_CONV = 4
RG_C = 8.0
D_HEADS = 4
MLSTM_CHUNK = 256
GATE_LANES = 128

VMEM_LIMIT_BYTES = 60 * 1024 * 1024
HALO_ROWS = 8
NORM_CHUNKS = 4
FFN_ROW_CHUNKS = 2


def _cparams(*sem):
    return pltpu.CompilerParams(dimension_semantics=sem, vmem_limit_bytes=VMEM_LIMIT_BYTES)


def _dot(a, b):
    return jnp.dot(a, b, preferred_element_type=F32)


def _dot_nt(a, b):
    return lax.dot_general(a, b, (((1,), (1,)), ((), ())), preferred_element_type=F32)


def _dot_tn(a, b):
    return lax.dot_general(a, b, (((0,), (0,)), ((), ())), preferred_element_type=F32)


def _norm_mod(x, g, scale, shift):
    xn = x * lax.rsqrt(jnp.mean(x * x, axis=-1, keepdims=True) + EPS)
    return (xn * g) * (1.0 + scale) + shift


def _mod_kernel(c_ref, w_ref, b_ref, o_ref):
    c = c_ref[...]
    cond = c * jax.nn.sigmoid(c)
    o_ref[...] = _dot(cond.astype(BF16), w_ref[...].astype(BF16)) + b_ref[...]


def _modulation(c, mod_w, mod_b):
    depth, d, n = mod_w.shape
    bsz = c.shape[0]
    rows = 8
    c8 = jnp.zeros((rows, d), F32).at[:bsz].set(c)
    tn = 1024
    out = pl.pallas_call(
        _mod_kernel,
        grid=(depth, n // tn),
        in_specs=[
            pl.BlockSpec((rows, d), lambda l, j: (0, 0)),
            pl.BlockSpec((None, d, tn), lambda l, j: (l, 0, j)),
            pl.BlockSpec((None, 1, tn), lambda l, j: (l, 0, j)),
        ],
        out_specs=pl.BlockSpec((None, rows, tn), lambda l, j: (l, 0, j)),
        out_shape=jax.ShapeDtypeStruct((depth, rows, n), F32),
        compiler_params=_cparams("arbitrary", "arbitrary"),
        name="adaln_modulation",
    )(c8, mod_w, mod_b.reshape(depth, 1, n))
    return out[:, :bsz].reshape(depth, bsz, 6, d)


def _in_proj_kernel(x_ref, g_ref, mod_ref, w_ref, *rest, shift_row, scale_row, with_gates):
    if with_gates:
        wg_ref, bg_ref, o_ref, og_ref, h_scr = rest
    else:
        o_ref, h_scr = rest
    j = pl.program_id(1)

    @pl.when(j == 0)
    def _():
        rc = x_ref.shape[0] // NORM_CHUNKS
        for r in range(NORM_CHUNKS):
            rows = slice(r * rc, (r + 1) * rc)
            h = _norm_mod(x_ref[rows, :], g_ref[...], mod_ref[scale_row:scale_row + 1, :],
                          mod_ref[shift_row:shift_row + 1, :])
            hb = h.astype(BF16)
            h_scr[rows, :] = hb
            o_ref[rows, :] = _dot(hb, w_ref[...]).astype(o_ref.dtype)
            if with_gates:
                og_ref[rows, :] = _dot(hb, wg_ref[...]) + bg_ref[...]

    @pl.when(j > 0)
    def _():
        o_ref[...] = _dot(h_scr[...], w_ref[...]).astype(o_ref.dtype)


def _in_proj(x2, g, mod_l, w, idx, n, seq, *, shift_row, scale_row, w_gate=None, b_gate=None, tm=1024, tn=1024):
    t, d = x2.shape
    tiles_per_seq = seq // tm
    with_gates = w_gate is not None
    in_specs = [
        pl.BlockSpec((tm, d), lambda i, j: (i, 0)),
        pl.BlockSpec((1, d), lambda i, j: (0, 0)),
        pl.BlockSpec((None, 6, d), lambda i, j: (i // tiles_per_seq, 0, 0)),
        pl.BlockSpec((None, d, tn), lambda i, j: (idx, 0, j)),
    ]
    args = [x2, g.reshape(1, d), mod_l, w]
    out_specs = pl.BlockSpec((tm, tn), lambda i, j: (i, j))
    out_shape = jax.ShapeDtypeStruct((t, n), BF16)
    if with_gates:
        in_specs += [pl.BlockSpec((d, GATE_LANES), lambda i, j: (0, 0)),
                     pl.BlockSpec((1, GATE_LANES), lambda i, j: (0, 0))]
        args += [w_gate, b_gate]
        out_specs = [out_specs, pl.BlockSpec((tm, GATE_LANES), lambda i, j: (i, 0))]
        out_shape = [out_shape, jax.ShapeDtypeStruct((t, GATE_LANES), F32)]
    return pl.pallas_call(
        functools.partial(_in_proj_kernel, shift_row=shift_row, scale_row=scale_row, with_gates=with_gates),
        grid=(t // tm, n // tn),
        in_specs=in_specs,
        out_specs=out_specs,
        out_shape=out_shape,
        scratch_shapes=[pltpu.VMEM((tm, d), BF16)],
        compiler_params=_cparams("parallel", "arbitrary"),
        name="norm_in_proj_gates" if with_gates else "norm_in_proj",
    )(*args)


def _out_proj_kernel(a_ref, b_ref, w_ref, x_ref, mod_ref, o_ref, *, gate_row, half):
    acc = _dot(a_ref[...], w_ref[0:half, :]) + _dot(b_ref[...], w_ref[half:, :])
    o_ref[...] = x_ref[...] + mod_ref[gate_row:gate_row + 1, :] * acc


def _out_proj(a, b, w, idx, x2, mod_l, seq, *, gate_row, tm=512):
    t, d = x2.shape
    half = a.shape[1]
    tiles_per_seq = seq // tm
    return pl.pallas_call(
        functools.partial(_out_proj_kernel, gate_row=gate_row, half=half),
        grid=(t // tm,),
        in_specs=[
            pl.BlockSpec((tm, half), lambda i: (i, 0)),
            pl.BlockSpec((tm, half), lambda i: (i, 0)),
            pl.BlockSpec((None, 2 * half, d), lambda i: (idx, 0, 0)),
            pl.BlockSpec((tm, d), lambda i: (i, 0)),
            pl.BlockSpec((None, 6, d), lambda i: (i // tiles_per_seq, 0, 0)),
        ],
        out_specs=pl.BlockSpec((tm, d), lambda i: (i, 0)),
        out_shape=jax.ShapeDtypeStruct((t, d), F32),
        compiler_params=_cparams("parallel"),
        name="out_proj_residual",
    )(a, b, w, x2, mod_l)


def _ffn_kernel(ws_ref, x_ref, g_ref, mod_ref, w1_ref, w3_ref, w2_ref, *rest, layer, final):
    if final:
        fg_ref, o_ref, h_scr, hs_scr = rest
    else:
        o_ref, h_scr, hs_scr = rest
    j = pl.program_id(1)
    tm = x_ref.shape[0]
    s1 = ws_ref[layer, 0]
    s3 = ws_ref[layer, 1]

    def hidden_tile(h8, row_scale):
        a = _dot(h8, w1_ref[...]) * (row_scale * s1)
        b = _dot(h8, w3_ref[...]) * (row_scale * s3)
        gated = (a * jax.nn.sigmoid(a)) * b
        return _dot(gated.astype(BF16), w2_ref[...])

    def row_chunks(n):
        rc = tm // n
        return [slice(r * rc, (r + 1) * rc) for r in range(n)]

    @pl.when(j == 0)
    def _():
        for rows in row_chunks(NORM_CHUNKS):
            h = _norm_mod(x_ref[rows, :], g_ref[...], mod_ref[4:5, :], mod_ref[3:4, :])
            amax = jnp.maximum(jnp.max(jnp.abs(h), axis=-1, keepdims=True), F8_TINY)
            h8 = (h * (F8_MAX / amax)).astype(F8)
            row_scale = amax * (1.0 / F8_MAX)
            h_scr[rows, :] = h8
            hs_scr[rows, :] = row_scale
            o_ref[rows, :] = hidden_tile(h8, row_scale)

    @pl.when(j > 0)
    def _():
        for rows in row_chunks(FFN_ROW_CHUNKS):
            o_ref[rows, :] += hidden_tile(h_scr[rows, :], hs_scr[rows, :])

    @pl.when(j == pl.num_programs(1) - 1)
    def _():
        for rows in row_chunks(FFN_ROW_CHUNKS):
            y = x_ref[rows, :] + mod_ref[5:6, :] * o_ref[rows, :]
            if final:
                y = y * lax.rsqrt(jnp.mean(y * y, axis=-1, keepdims=True) + EPS) * fg_ref[...]
            o_ref[rows, :] = y


def _fp8_weights(w):
    amax = jnp.maximum(jnp.max(jnp.abs(w), axis=(1, 2)), F8_TINY)
    scale = amax * (1.0 / F8_MAX)
    return (w * (F8_MAX / amax)[:, None, None]).astype(F8), scale


def _ffn(x2, g, mod_l, w1, w3, w_scales, w2, layer, seq, final_g=None, *, tm=1024, tf=512):
    t, d = x2.shape
    f = w1.shape[2]
    tiles_per_seq = seq // tm
    final = final_g is not None
    in_specs = [
        pl.BlockSpec(memory_space=pltpu.SMEM),
        pl.BlockSpec((tm, d), lambda i, j: (i, 0)),
        pl.BlockSpec((1, d), lambda i, j: (0, 0)),
        pl.BlockSpec((None, 6, d), lambda i, j: (i // tiles_per_seq, 0, 0)),
        pl.BlockSpec((None, d, tf), lambda i, j: (layer, 0, j)),
        pl.BlockSpec((None, d, tf), lambda i, j: (layer, 0, j)),
        pl.BlockSpec((None, tf, d), lambda i, j: (layer, j, 0)),
    ]
    args = [w_scales, x2, g.reshape(1, d), mod_l, w1, w3, w2]
    if final:
        in_specs.append(pl.BlockSpec((1, d), lambda i, j: (0, 0)))
        args.append(final_g.reshape(1, d))
    return pl.pallas_call(
        functools.partial(_ffn_kernel, layer=layer, final=final),
        grid=(t // tm, f // tf),
        in_specs=in_specs,
        out_specs=pl.BlockSpec((tm, d), lambda i, j: (i, 0)),
        out_shape=jax.ShapeDtypeStruct((t, d), F32),
        scratch_shapes=[pltpu.VMEM((tm, d), F8), pltpu.VMEM((tm, 1), F32)],
        compiler_params=_cparams("parallel", "arbitrary"),
        name="swiglu_ffn_final" if final else "swiglu_ffn",
    )(*args)


def _gated_conv_kernel(gb_ref, gc_ref, xb_ref, gch_ref, xbh_ref, w_ref, o_ref, u_scr, *, tiles_per_seq, tm):
    i = pl.program_id(0)
    u = gc_ref[...].astype(F32) * xb_ref[...].astype(F32)
    uh = gch_ref[...].astype(F32) * xbh_ref[...].astype(F32)
    first = (i % tiles_per_seq) == 0
    u_scr[0:HALO_ROWS, :] = jnp.where(first, 0.0, uh)
    u_scr[HALO_ROWS:, :] = u
    w = w_ref[...]
    conv = w[2:3, :] * u
    conv += w[1:2, :] * u_scr[HALO_ROWS - 1:HALO_ROWS - 1 + tm, :]
    conv += w[0:1, :] * u_scr[HALO_ROWS - 2:HALO_ROWS - 2 + tm, :]
    o_ref[...] = (gb_ref[...].astype(F32) * conv).astype(o_ref.dtype)


def _gated_conv(proj, conv_w, seq, *, col0, tm=1024):
    t = proj.shape[0]
    cw = conv_w.shape[1]
    cb = col0 // cw
    tiles_per_seq = seq // tm
    hb = tm // HALO_ROWS

    def halo_map(k):
        return lambda i: (jnp.maximum(i * hb - 1, 0), cb + k)

    return pl.pallas_call(
        functools.partial(_gated_conv_kernel, tiles_per_seq=tiles_per_seq, tm=tm),
        grid=(t // tm,),
        in_specs=[
            pl.BlockSpec((tm, cw), lambda i: (i, cb)),
            pl.BlockSpec((tm, cw), lambda i: (i, cb + 1)),
            pl.BlockSpec((tm, cw), lambda i: (i, cb + 2)),
            pl.BlockSpec((HALO_ROWS, cw), halo_map(1)),
            pl.BlockSpec((HALO_ROWS, cw), halo_map(2)),
            pl.BlockSpec((B_CONV, cw), lambda i: (0, 0)),
        ],
        out_specs=pl.BlockSpec((tm, cw), lambda i: (i, 0)),
        out_shape=jax.ShapeDtypeStruct((t, cw), BF16),
        scratch_shapes=[pltpu.VMEM((tm + HALO_ROWS, cw), F32)],
        compiler_params=_cparams("parallel"),
        name="gated_short_conv",
    )(proj, proj, proj, proj, proj, conv_w)


def _split3(x):
    hi = x.astype(BF16).astype(F32)
    mid = (x - hi).astype(BF16).astype(F32)
    lo = (x - hi - mid).astype(BF16).astype(F32)
    return hi, mid, lo


def _moba_prep_kernel(slopes_ref, q_ref, k_ref, v_ref, qx_ref, kx_ref, vt_ref, *, nb, blk, grp):
    s, dh = q_ref.shape
    u = slopes_ref[pl.program_id(1)] * (dh ** 0.5)
    kmean = jnp.mean(k_ref[...].astype(F32).reshape(nb, blk, dh), axis=1)
    hi = kmean.astype(BF16)
    lo = (kmean - hi.astype(F32)).astype(BF16)
    q = q_ref[...]
    gate = _dot_nt(hi, q) + _dot_nt(lo, q)
    lanes_per = RANK_QBLOCKS * blk
    pieces = []
    for c in range(s // lanes_per):
        nr = min(nb, RANK_QBLOCKS * (c + 1))
        n_idx = lax.broadcasted_iota(jnp.int32, (nr, lanes_per), 0)
        q_blk = lax.broadcasted_iota(jnp.int32, (nr, lanes_per), 1) // blk + c * RANK_QBLOCKS
        eligible = n_idx < q_blk
        g = jnp.where(eligible, gate[0:nr, c * lanes_per:(c + 1) * lanes_per], NEG_INF)
        rank = jnp.zeros((nr, lanes_per), jnp.int32)
        for m in range(nr):
            gm = g[m:m + 1, :]
            beats = (gm > g) | ((gm == g) & (n_idx > m))
            rank += beats.astype(jnp.int32)
        keep = (eligible & (rank < MOBA_TOPK)) | (n_idx == q_blk)
        piece = jnp.where(keep, 0.0, NEG_INF)
        if nr < nb:
            piece = jnp.concatenate([piece, jnp.full((nb - nr, lanes_per), NEG_INF, F32)], axis=0)
        pieces.append(piece)
    sel_bias = jnp.concatenate(pieces, axis=1) if len(pieces) > 1 else pieces[0]

    off = (lax.broadcasted_iota(jnp.int32, (8, s), 1) % blk).astype(F32) * u
    hi, mid, lo = _split3(off)
    feat = lax.broadcasted_iota(jnp.int32, (8, s), 0)
    pos_rows = jnp.where(feat == 0, hi, jnp.where(feat == 1, mid, jnp.where(feat == 2, lo,
                         jnp.where(feat < 6, 1.0, 0.0))))
    ext_t = jnp.concatenate([sel_bias, pos_rows, jnp.zeros((dh - nb - 8, s), F32)], axis=0)
    chunk = grp * blk
    for ci in range(s // chunk):
        rows = slice(ci * chunk, (ci + 1) * chunk)
        qx_ref[rows, 0:dh] = q_ref[rows, :]
        qx_ref[rows, dh:2 * dh] = ext_t[:, rows].T.astype(qx_ref.dtype)

    col = lax.broadcasted_iota(jnp.int32, (blk, dh), 1)
    khi, kmid, klo = _split3(lax.broadcasted_iota(jnp.int32, (blk, dh), 0).astype(F32) * u)
    tile = jnp.where((col >= nb) & (col < nb + 3), -1.0,
                     jnp.where(col == nb + 3, khi, jnp.where(col == nb + 4, kmid,
                               jnp.where(col == nb + 5, klo, 0.0))))
    kx_ref[:, 0:dh] = k_ref[...]
    for n in range(nb):
        kx_ref[n * blk:(n + 1) * blk, dh:2 * dh] = jnp.where(col == n, 1.0, tile).astype(kx_ref.dtype)

    for n in range(nb):
        v_t = v_ref[n * blk:(n + 1) * blk, :].astype(F32).T.astype(vt_ref.dtype)
        vt_ref[n // grp, :, (n % grp) * blk:(n % grp + 1) * blk] = v_t


def _moba_attn_kernel(slopes_ref, qx_ref, kx_ref, vt_ref, o_ref, m_scr, l_scr, acc_scr, *, blk, grp, scale):
    h = pl.program_id(1)
    qt = pl.program_id(2)
    slope2 = slopes_ref[h] * LOG2E
    c2 = scale * LOG2E
    causal = (lax.broadcasted_iota(jnp.int32, (blk, blk), 0)
              <= lax.broadcasted_iota(jnp.int32, (blk, blk), 1))

    def scores(b, kg):
        return _dot_nt(kg, qx_ref[b * blk:(b + 1) * blk, :])

    def update(b, raw, vt_g, n0, n_sub, first):
        q_blk = qt * grp + b
        subs, shifts = [], []
        m_new = None if first else m_scr[b]
        for gi in range(n_sub):
            r = raw[gi * blk:(gi + 1) * blk, :]
            if first and gi == b:
                r = jnp.where(causal, r, NEG_INF)
            shift = -slope2 * (blk * (q_blk - (n0 + gi))).astype(F32)
            cm = jnp.max(r, axis=0, keepdims=True) * c2 + shift
            m_new = cm if m_new is None else jnp.maximum(m_new, cm)
            subs.append(r)
            shifts.append(shift)
        l = None
        probs = []
        for r, shift in zip(subs, shifts):
            p = jnp.exp2(r * c2 - (m_new - shift))
            ps = jnp.sum(p, axis=0, keepdims=True)
            l = ps if l is None else l + ps
            probs.append(p.astype(BF16))
        pv = _dot(vt_g, jnp.concatenate(probs, axis=0) if n_sub > 1 else probs[0])
        if first:
            acc_scr[b] = pv
        else:
            alpha = jnp.exp2(m_scr[b] - m_new)
            l = alpha * l_scr[b] + l
            acc_scr[b] = alpha * acc_scr[b] + pv
        m_scr[b] = m_new
        l_scr[b] = l


    base = pl.multiple_of(qt * (grp * blk), grp * blk)
    raws = [scores(b, kx_ref[pl.ds(base, (b + 1) * blk), :]) for b in range(grp)]
    for b in range(grp):
        update(b, raws[b], vt_ref[qt, :, 0:(b + 1) * blk], qt * grp, b + 1, True)

    def body(g, carry):
        kg = kx_ref[pl.ds(pl.multiple_of(g * (grp * blk), grp * blk), grp * blk), :]
        vt_g = vt_ref[g]
        raws = [scores(b, kg) for b in range(grp)]
        for b in range(grp):
            update(b, raws[b], vt_g, g * grp, grp, False)
        return carry

    lax.fori_loop(0, qt, body, 0)
    for b in range(grp):
        o_ref[b * blk:(b + 1) * blk, :] = (acc_scr[b] / l_scr[b]).T.astype(o_ref.dtype)


def _moba(proj3, slopes, n_heads):
    bsz, s, _ = proj3.shape
    dh, blk, grp = A_HEAD_DIM, MOBA_BLOCK, MOBA_GROUP
    nb = s // blk
    ng = nb // grp
    tq = grp * blk
    smem = pl.BlockSpec(memory_space=pltpu.SMEM)
    qx, kx, vt = pl.pallas_call(
        functools.partial(_moba_prep_kernel, nb=nb, blk=blk, grp=grp),
        grid=(bsz, n_heads),
        in_specs=[
            smem,
            pl.BlockSpec((None, s, dh), lambda b, h: (b, 0, h)),
            pl.BlockSpec((None, s, dh), lambda b, h: (b, 0, n_heads + h)),
            pl.BlockSpec((None, s, dh), lambda b, h: (b, 0, 2 * n_heads + h)),
        ],
        out_specs=[
            pl.BlockSpec((None, None, s, 2 * dh), lambda b, h: (b, h, 0, 0)),
            pl.BlockSpec((None, None, s, 2 * dh), lambda b, h: (b, h, 0, 0)),
            pl.BlockSpec((None, None, ng, dh, tq), lambda b, h: (b, h, 0, 0, 0)),
        ],
        out_shape=[
            jax.ShapeDtypeStruct((bsz, n_heads, s, 2 * dh), BF16),
            jax.ShapeDtypeStruct((bsz, n_heads, s, 2 * dh), BF16),
            jax.ShapeDtypeStruct((bsz, n_heads, ng, dh, tq), BF16),
        ],
        compiler_params=_cparams("parallel", "parallel"),
        name="moba_select",
    )(slopes, proj3, proj3, proj3)
    return pl.pallas_call(
        functools.partial(_moba_attn_kernel, blk=blk, grp=grp, scale=dh ** -0.5),
        grid=(bsz, n_heads, ng),
        in_specs=[
            smem,
            pl.BlockSpec((None, None, tq, 2 * dh), lambda b, h, i: (b, h, i, 0)),
            pl.BlockSpec((None, None, s, 2 * dh), lambda b, h, i: (b, h, 0, 0)),
            pl.BlockSpec((None, None, ng, dh, tq), lambda b, h, i: (b, h, 0, 0, 0)),
        ],
        out_specs=pl.BlockSpec((None, tq, dh), lambda b, h, i: (b, i, h)),
        out_shape=jax.ShapeDtypeStruct((bsz, s, n_heads * dh), BF16),
        scratch_shapes=[pltpu.VMEM((grp, 1, blk), F32), pltpu.VMEM((grp, 1, blk), F32),
                        pltpu.VMEM((grp, dh, blk), F32)],
        compiler_params=_cparams("parallel", "parallel", "arbitrary"),
        name="moba_attention",
    )(slopes, qx, kx, vt)


def _shift_rows(x, d, fill):
    n, c = x.shape
    if d % HALO_ROWS == 0:
        return jnp.concatenate([jnp.full((d, c), fill, x.dtype), x[:n - d]], axis=0)
    rolled = pltpu.roll(x, d, axis=0)
    rows = lax.broadcasted_iota(jnp.int32, x.shape, 0)
    return jnp.where(rows < d, fill, rolled)


def _rglru_kernel(x_ref, xh_ref, gr_ref, cw_ref, cb_ref, w_ref, ba_ref, bx_ref, lam_ref, o_ref,
                  h_scr, *, ts, n_blocks):
    si = pl.program_id(1)

    @pl.when(si == 0)
    def _():
        h_scr[...] = jnp.zeros_like(h_scr)

    xb = x_ref[...]
    cw = cw_ref[...]
    halo = jnp.where(si == 0, 0.0, xh_ref[...].astype(F32))
    halo_row = lax.broadcasted_iota(jnp.int32, halo.shape, 0)
    delta = (lax.broadcasted_iota(jnp.int32, (ts, ts), 0) - lax.broadcasted_iota(jnp.int32, (ts, ts), 1))
    xc = cw[C_CONV - 1:C_CONV, :] * xb.astype(F32) + cb_ref[...]
    head = jnp.zeros_like(halo)
    for j in range(C_CONV - 1):
        d = C_CONV - 1 - j
        shifted = _dot(jnp.where(delta == d, 1.0, 0.0).astype(BF16), xb)
        xc += cw[j:j + 1, :] * shifted
        head += cw[j:j + 1, :] * jnp.where(halo_row < d, pltpu.roll(halo, d, axis=0), 0.0)
    xc = jnp.concatenate([xc[:HALO_ROWS] + head, xc[HALO_ROWS:]], axis=0)

    bd = xc.shape[1] // n_blocks
    xcb = xc.astype(BF16)
    gates = [_dot(xcb[:, g * bd:(g + 1) * bd], w_ref[g]) for g in range(n_blocks)]
    r = jax.nn.sigmoid(jnp.concatenate([gt[:, :bd] for gt in gates], axis=1) + ba_ref[...])
    i = jax.nn.sigmoid(jnp.concatenate([gt[:, bd:] for gt in gates], axis=1) + bx_ref[...])
    log_a = (-RG_C) * r * jax.nn.softplus(-lam_ref[...])
    a = jnp.exp(log_a)
    z = -jnp.tanh(log_a) * (a * a + 1.0)
    u = jnp.where(z > 0.0, z * lax.rsqrt(z), 0.0) * (i * xc)

    d = 1
    while d < ts:
        u = a * _shift_rows(u, d, 0.0) + u
        a = a * _shift_rows(a, d, 1.0)
        d *= 2
    hs = a * h_scr[...] + u
    h_scr[...] = hs[ts - 1:ts, :]
    o_ref[...] = (hs * jax.nn.gelu(gr_ref[...].astype(F32))).astype(o_ref.dtype)


def _rglru(proj3, conv_w, conv_b, w_cat, ba, bx, lam, *, ts=256):
    bsz, s, _ = proj3.shape
    c = conv_w.shape[1]
    n_blocks, bd, _ = w_cat.shape
    hb = ts // HALO_ROWS
    vec = pl.BlockSpec((1, c), lambda b, i: (0, 0))
    return pl.pallas_call(
        functools.partial(_rglru_kernel, ts=ts, n_blocks=n_blocks),
        grid=(bsz, s // ts),
        in_specs=[
            pl.BlockSpec((None, ts, c), lambda b, i: (b, i, 0)),
            pl.BlockSpec((None, HALO_ROWS, c), lambda b, i: (b, jnp.maximum(i * hb - 1, 0), 0)),
            pl.BlockSpec((None, ts, c), lambda b, i: (b, i, 1)),
            pl.BlockSpec((C_CONV, c), lambda b, i: (0, 0)),
            vec,
            pl.BlockSpec((n_blocks, bd, 2 * bd), lambda b, i: (0, 0, 0)),
            vec, vec, vec,
        ],
        out_specs=pl.BlockSpec((None, ts, c), lambda b, i: (b, i, 0)),
        out_shape=jax.ShapeDtypeStruct((bsz, s, c), BF16),
        scratch_shapes=[pltpu.VMEM((1, c), F32)],
        compiler_params=_cparams("parallel", "arbitrary"),
        name="rglru_scan",
    )(proj3, proj3, proj3, conv_w, conv_b.reshape(1, c), w_cat, ba.reshape(1, c), bx.reshape(1, c),
      lam.reshape(1, c))


def _mlstm_kernel(q_ref, k_ref, v_ref, og_ref, gt_ref, o_ref, c_scr, n_scr, m_scr, *, chunk, nh, dh):
    ci = pl.program_id(1)

    @pl.when(ci == 0)
    def _():
        c_scr[...] = jnp.zeros_like(c_scr)
        n_scr[...] = jnp.zeros_like(n_scr)
        m_scr[...] = jnp.zeros_like(m_scr)

    gates = gt_ref[...]
    bcum = jax.nn.log_sigmoid(gates)
    d = 1
    while d < chunk:
        bcum = bcum + _shift_rows(bcum, d, 0.0)
        d *= 2
    lane = lax.broadcasted_iota(jnp.int32, gates.shape, 1)
    rows_t = jnp.where(lane < nh, gates, bcum).T
    r_idx = lax.broadcasted_iota(jnp.int32, (chunk, chunk), 0)
    c_idx = lax.broadcasted_iota(jnp.int32, (chunk, chunk), 1)
    causal = c_idx <= r_idx
    k_scale = dh ** -0.5

    for h in range(nh):
        cols = slice(h * dh, (h + 1) * dh)
        i_col = gates[:, h:h + 1]
        b_col = bcum[:, nh + h:nh + h + 1]
        i_row = rows_t[h:h + 1, :]
        b_row = rows_t[nh + h:nh + h + 1, :]
        m_prev = m_scr[h]

        dmat = jnp.where(causal, b_col - b_row + i_row, NEG_INF)
        inter = b_col + m_prev
        m_t = jnp.maximum(inter, jnp.max(dmat, axis=-1, keepdims=True))
        w_intra = jnp.exp(dmat - m_t)
        w_inter = jnp.exp(inter - m_t)

        q = q_ref[:, cols]
        ks = k_ref[:, cols].astype(F32) * k_scale
        ksb = ks.astype(BF16)
        v = v_ref[:, cols]
        s_qk = _dot_nt(q, ksb) * w_intra
        c_prev = c_scr[h]
        n_prev = n_scr[h]
        num = w_inter * _dot(q, c_prev.astype(BF16)) + _dot(s_qk.astype(BF16), v)
        q_n = jnp.sum(q.astype(F32) * n_prev, axis=-1, keepdims=True)
        den = w_inter * q_n + jnp.sum(s_qk, axis=-1, keepdims=True)
        hid = num / jnp.maximum(jnp.abs(den), jnp.exp(-m_t))

        b_last = b_col[chunk - 1:chunk, :]
        w_s = b_last - b_col + i_col
        m_new = jnp.maximum(b_last + m_prev, jnp.max(w_s, axis=0, keepdims=True))
        decay = jnp.exp(b_last + m_prev - m_new)
        kw = jnp.exp(w_s - m_new) * ks
        c_scr[h] = decay * c_prev + _dot_tn(kw.astype(BF16), v)
        n_scr[h] = decay * n_prev + jnp.sum(kw, axis=0, keepdims=True)
        m_scr[h] = m_new

        o_ref[:, cols] = (hid * jax.nn.sigmoid(og_ref[:, cols].astype(F32))).astype(o_ref.dtype)


def _mlstm(proj3, gates3, *, col0, nh, chunk=MLSTM_CHUNK):
    bsz, s, _ = proj3.shape
    dh = 256
    w = nh * dh
    cb = col0 // w
    blk = lambda k: pl.BlockSpec((None, chunk, w), lambda b, i: (b, i, cb + k))
    return pl.pallas_call(
        functools.partial(_mlstm_kernel, chunk=chunk, nh=nh, dh=dh),
        grid=(bsz, s // chunk),
        in_specs=[blk(0), blk(1), blk(2), blk(3),
                  pl.BlockSpec((None, chunk, GATE_LANES), lambda b, i: (b, i, 0))],
        out_specs=pl.BlockSpec((None, chunk, w), lambda b, i: (b, i, 0)),
        out_shape=jax.ShapeDtypeStruct((bsz, s, w), BF16),
        scratch_shapes=[pltpu.VMEM((nh, dh, dh), F32), pltpu.VMEM((nh, 1, dh), F32),
                        pltpu.VMEM((nh, 1, 1), F32)],
        compiler_params=_cparams("parallel", "arbitrary"),
        name="mlstm_chunkwise",
    )(proj3, proj3, proj3, proj3, gates3)


def kernel(x, c, mod_w, mod_b, norm_mix_g, norm_ffn_g, ffn_w1, ffn_w3, ffn_w2, ab_w_in, ab_conv_w, ab_w_out, cd_w_in, cd_gate_b, rg_conv_w, rg_conv_b, rg_wa, rg_ba, rg_wx, rg_bx, rg_lambda, cd_w_out, final_norm_g):
    bsz, seq, d = x.shape
    depth = mod_w.shape[0]
    t = bsz * seq
    half = d // 2
    a_heads = half // A_HEAD_DIM
    slopes = 2.0 ** (-8.0 * jnp.arange(1, a_heads + 1, dtype=F32) / a_heads)

    mod = _modulation(c, mod_w, mod_b)
    ffn_w1_q, ffn_s1 = _fp8_weights(ffn_w1)
    ffn_w3_q, ffn_s3 = _fp8_weights(ffn_w3)
    ffn_scales = jnp.stack([ffn_s1, ffn_s3], axis=1)
    ffn_w2_b = ffn_w2.astype(BF16)
    ab_w_in_b, ab_w_out_b = ab_w_in.astype(BF16), ab_w_out.astype(BF16)
    cd_w_in_b, cd_w_out_b = cd_w_in.astype(BF16), cd_w_out.astype(BF16)
    x2 = x.reshape(t, d)
    for layer in range(depth):
        mod_l = mod[layer]
        if layer % 2 == 0:
            e = layer // 2
            n_main = 6 * half
            proj = _in_proj(x2, norm_mix_g[layer], mod_l, ab_w_in_b, e, n_main, seq, shift_row=0, scale_row=1)
            proj3 = proj.reshape(bsz, seq, n_main)
            y_a = _moba(proj3, slopes, a_heads).reshape(t, half)
            y_b = _gated_conv(proj, ab_conv_w[e], seq, col0=3 * half)
            w_out, w_idx = ab_w_out_b, e
        else:
            o = layer // 2
            n_main = 2 * half + 4 * half
            n_g = cd_w_in.shape[2] - n_main
            w_gate = jnp.zeros((d, GATE_LANES), BF16).at[:, :n_g].set(cd_w_in[o, :, n_main:].astype(BF16))
            b_gate = jnp.zeros((1, GATE_LANES), F32).at[0, :n_g].set(cd_gate_b[o])
            proj, gates = _in_proj(x2, norm_mix_g[layer], mod_l, cd_w_in_b, o, n_main, seq,
                                   shift_row=0, scale_row=1, w_gate=w_gate, b_gate=b_gate)
            proj3 = proj.reshape(bsz, seq, n_main)
            w_cat = jnp.concatenate([rg_wa[o], rg_wx[o]], axis=-1).astype(BF16)
            y_a = _rglru(proj3, rg_conv_w[o], rg_conv_b[o], w_cat, rg_ba[o], rg_bx[o],
                         rg_lambda[o]).reshape(t, half)
            y_b = _mlstm(proj3, gates.reshape(bsz, seq, GATE_LANES), col0=2 * half,
                         nh=D_HEADS).reshape(t, half)
            w_out, w_idx = cd_w_out_b, o
        x2 = _out_proj(y_a, y_b, w_out, w_idx, x2, mod_l, seq, gate_row=2)
        final_g = final_norm_g if layer == depth - 1 else None
        x2 = _ffn(x2, norm_ffn_g[layer], mod_l, ffn_w1_q, ffn_w3_q, ffn_scales, ffn_w2_b, layer, seq, final_g)
    return x2.reshape(bsz, seq, d)
```

```python
import functools

import jax
import jax.numpy as jnp
from jax import lax
from jax.experimental import pallas as pl
from jax.experimental.pallas import tpu as pltpu

F32 = jnp.float32
BF16 = jnp.bfloat16
F8 = jnp.float8_e4m3fn
F8_MAX = 448.0
F8_TINY = 1e-30

EPS = 1e-6
NEG_INF = -1e30
LOG2E = 1.4426950408889634

A_HEAD_DIM = 128
MOBA_BLOCK = 256
MOBA_TOPK = 3
MOBA_GROUP = 4
RANK_QBLOCKS = 8
B_CONV = 3
C_BLOCKS = 8
C_CONV = 4
RG_C = 8.0
D_HEADS = 4
MLSTM_CHUNK = 256
GATE_LANES = 128

VMEM_LIMIT_BYTES = 60 * 1024 * 1024
HALO_ROWS = 8
NORM_CHUNKS = 4
FFN_ROW_CHUNKS = 2


def _cparams(*sem):
    return pltpu.CompilerParams(dimension_semantics=sem, vmem_limit_bytes=VMEM_LIMIT_BYTES)


def _dot(a, b):
    return jnp.dot(a, b, preferred_element_type=F32)


def _dot_nt(a, b):
    return lax.dot_general(a, b, (((1,), (1,)), ((), ())), preferred_element_type=F32)


def _dot_tn(a, b):
    return lax.dot_general(a, b, (((0,), (0,)), ((), ())), preferred_element_type=F32)


def _norm_mod(x, g, scale, shift):
    xn = x * lax.rsqrt(jnp.mean(x * x, axis=-1, keepdims=True) + EPS)
    return (xn * g) * (1.0 + scale) + shift


def _mod_kernel(c_ref, w_ref, b_ref, o_ref):
    c = c_ref[...]
    cond = c * jax.nn.sigmoid(c)
    o_ref[...] = _dot(cond.astype(BF16), w_ref[...].astype(BF16)) + b_ref[...]


def _modulation(c, mod_w, mod_b):
    depth, d, n = mod_w.shape
    bsz = c.shape[0]
    rows = 8
    c8 = jnp.zeros((rows, d), F32).at[:bsz].set(c)
    tn = 1024
    out = pl.pallas_call(
        _mod_kernel,
        grid=(depth, n // tn),
        in_specs=[
            pl.BlockSpec((rows, d), lambda l, j: (0, 0)),
            pl.BlockSpec((None, d, tn), lambda l, j: (l, 0, j)),
            pl.BlockSpec((None, 1, tn), lambda l, j: (l, 0, j)),
        ],
        out_specs=pl.BlockSpec((None, rows, tn), lambda l, j: (l, 0, j)),
        out_shape=jax.ShapeDtypeStruct((depth, rows, n), F32),
        compiler_params=_cparams("arbitrary", "arbitrary"),
        name="adaln_modulation",
    )(c8, mod_w, mod_b.reshape(depth, 1, n))
    return out[:, :bsz].reshape(depth, bsz, 6, d)


def _in_proj_kernel(x_ref, g_ref, mod_ref, w_ref, *rest, shift_row, scale_row, with_gates):
    if with_gates:
        wg_ref, bg_ref, o_ref, og_ref, h_scr = rest
    else:
        o_ref, h_scr = rest
    j = pl.program_id(1)

    @pl.when(j == 0)
    def _():
        rc = x_ref.shape[0] // NORM_CHUNKS
        for r in range(NORM_CHUNKS):
            rows = slice(r * rc, (r + 1) * rc)
            h = _norm_mod(x_ref[rows, :], g_ref[...], mod_ref[scale_row:scale_row + 1, :],
                          mod_ref[shift_row:shift_row + 1, :])
            hb = h.astype(BF16)
            h_scr[rows, :] = hb
            o_ref[rows, :] = _dot(hb, w_ref[...]).astype(o_ref.dtype)
            if with_gates:
                og_ref[rows, :] = _dot(hb, wg_ref[...]) + bg_ref[...]

    @pl.when(j > 0)
    def _():
        o_ref[...] = _dot(h_scr[...], w_ref[...]).astype(o_ref.dtype)


def _in_proj(x2, g, mod_l, w, idx, n, seq, *, shift_row, scale_row, w_gate=None, b_gate=None, tm=1024, tn=2048):
    t, d = x2.shape
    tiles_per_seq = seq // tm
    with_gates = w_gate is not None
    in_specs = [
        pl.BlockSpec((tm, d), lambda i, j: (i, 0)),
        pl.BlockSpec((1, d), lambda i, j: (0, 0)),
        pl.BlockSpec((None, 6, d), lambda i, j: (i // tiles_per_seq, 0, 0)),
        pl.BlockSpec((None, d, tn), lambda i, j: (idx, 0, j)),
    ]
    args = [x2, g.reshape(1, d), mod_l, w]
    out_specs = pl.BlockSpec((tm, tn), lambda i, j: (i, j))
    out_shape = jax.ShapeDtypeStruct((t, n), BF16)
    if with_gates:
        in_specs += [pl.BlockSpec((d, GATE_LANES), lambda i, j: (0, 0)),
                     pl.BlockSpec((1, GATE_LANES), lambda i, j: (0, 0))]
        args += [w_gate, b_gate]
        out_specs = [out_specs, pl.BlockSpec((tm, GATE_LANES), lambda i, j: (i, 0))]
        out_shape = [out_shape, jax.ShapeDtypeStruct((t, GATE_LANES), F32)]
    return pl.pallas_call(
        functools.partial(_in_proj_kernel, shift_row=shift_row, scale_row=scale_row, with_gates=with_gates),
        grid=(t // tm, n // tn),
        in_specs=in_specs,
        out_specs=out_specs,
        out_shape=out_shape,
        scratch_shapes=[pltpu.VMEM((tm, d), BF16)],
        compiler_params=_cparams("parallel", "arbitrary"),
        name="norm_in_proj_gates" if with_gates else "norm_in_proj",
    )(*args)


def _gated_conv_tile(gb_ref, gc_ref, xb_ref, gch_ref, xbh_ref, w_ref, u_scr, first):
    tm = gc_ref.shape[0]
    u_scr[0:HALO_ROWS, :] = jnp.where(first, 0.0, gch_ref[...].astype(F32) * xbh_ref[...].astype(F32))
    u = gc_ref[...].astype(F32) * xb_ref[...].astype(F32)
    u_scr[HALO_ROWS:, :] = u
    w = w_ref[...]
    conv = w[B_CONV - 1:B_CONV, :] * u
    for j in range(B_CONV - 1):
        off = HALO_ROWS - (B_CONV - 1) + j
        conv += w[j:j + 1, :] * u_scr[off:off + tm, :]
    return gb_ref[...].astype(F32) * conv


def _out_proj_kernel(a_ref, *rest, gate_row, half, conv_tiles_per_seq):
    if conv_tiles_per_seq:
        gb_ref, gc_ref, xb_ref, gch_ref, xbh_ref, cw_ref, w_ref, x_ref, mod_ref, o_ref, u_scr = rest
    else:
        b_ref, w_ref, x_ref, mod_ref, o_ref = rest
    acc = _dot(a_ref[...], w_ref[0:half, :])
    if conv_tiles_per_seq:
        first = (pl.program_id(0) % conv_tiles_per_seq) == 0
        b = _gated_conv_tile(gb_ref, gc_ref, xb_ref, gch_ref, xbh_ref, cw_ref, u_scr, first).astype(BF16)
    else:
        b = b_ref[...]
    acc += _dot(b, w_ref[half:, :])
    o_ref[...] = x_ref[...] + mod_ref[gate_row:gate_row + 1, :] * acc


def _out_proj(a, b, w, idx, x2, mod_l, seq, *, gate_row, conv_w=None, conv_col0=None, tm=512):
    t, d = x2.shape
    half = a.shape[1]
    tiles_per_seq = seq // tm
    fused_conv = conv_w is not None
    row_block = pl.BlockSpec((tm, half), lambda i: (i, 0))
    if fused_conv:
        cb = conv_col0 // half
        hb = tm // HALO_ROWS
        halo = lambda k: pl.BlockSpec((HALO_ROWS, half), lambda i: (jnp.maximum(i * hb - 1, 0), cb + k))
        b_specs = [pl.BlockSpec((tm, half), lambda i: (i, cb)), pl.BlockSpec((tm, half), lambda i: (i, cb + 1)),
                   pl.BlockSpec((tm, half), lambda i: (i, cb + 2)), halo(1), halo(2),
                   pl.BlockSpec((B_CONV, half), lambda i: (0, 0))]
        b_args = [b, b, b, b, b, conv_w]
        scratch = [pltpu.VMEM((tm + HALO_ROWS, half), F32)]
    else:
        b_specs, b_args, scratch = [row_block], [b], []
    return pl.pallas_call(
        functools.partial(_out_proj_kernel, gate_row=gate_row, half=half,
                          conv_tiles_per_seq=tiles_per_seq if fused_conv else 0),
        grid=(t // tm,),
        in_specs=[row_block] + b_specs + [
            pl.BlockSpec((None, 2 * half, d), lambda i: (idx, 0, 0)),
            pl.BlockSpec((tm, d), lambda i: (i, 0)),
            pl.BlockSpec((None, 6, d), lambda i: (i // tiles_per_seq, 0, 0)),
        ],
        out_specs=pl.BlockSpec((tm, d), lambda i: (i, 0)),
        out_shape=jax.ShapeDtypeStruct((t, d), F32),
        scratch_shapes=scratch,
        compiler_params=_cparams("parallel"),
        name="out_proj_conv_residual" if fused_conv else "out_proj_residual",
    )(a, *b_args, w, x2, mod_l)


def _ffn_kernel(x_ref, g_ref, mod_ref, w1_ref, w3_ref, s1_ref, s3_ref, w2_ref, *rest, final):
    if final:
        fg_ref, o_ref, h_scr, hs_scr = rest
    else:
        o_ref, h_scr, hs_scr = rest
    j = pl.program_id(1)
    tm = x_ref.shape[0]

    def hidden_tile(h8, row_scale):
        a = _dot(h8, w1_ref[...]) * row_scale * s1_ref[...]
        b = _dot(h8, w3_ref[...]) * row_scale * s3_ref[...]
        gated = (a * jax.nn.sigmoid(a)) * b
        return _dot(gated.astype(BF16), w2_ref[...])

    def row_chunks(n):
        rc = tm // n
        return [slice(r * rc, (r + 1) * rc) for r in range(n)]

    @pl.when(j == 0)
    def _():
        for rows in row_chunks(NORM_CHUNKS):
            h = _norm_mod(x_ref[rows, :], g_ref[...], mod_ref[4:5, :], mod_ref[3:4, :])
            amax = jnp.maximum(jnp.max(jnp.abs(h), axis=-1, keepdims=True), F8_TINY)
            h8 = (h * (F8_MAX / amax)).astype(F8)
            row_scale = amax * (1.0 / F8_MAX)
            h_scr[rows, :] = h8
            hs_scr[rows, :] = row_scale
            o_ref[rows, :] = hidden_tile(h8, row_scale)

    @pl.when(j > 0)
    def _():
        for rows in row_chunks(FFN_ROW_CHUNKS):
            o_ref[rows, :] += hidden_tile(h_scr[rows, :], hs_scr[rows, :])

    @pl.when(j == pl.num_programs(1) - 1)
    def _():
        for rows in row_chunks(FFN_ROW_CHUNKS):
            y = x_ref[rows, :] + mod_ref[5:6, :] * o_ref[rows, :]
            if final:
                y = y * lax.rsqrt(jnp.mean(y * y, axis=-1, keepdims=True) + EPS) * fg_ref[...]
            o_ref[rows, :] = y


def _fp8_weights_kernel(w_ref, q_ref, s_ref):
    w = w_ref[...]
    amax = jnp.maximum(jnp.max(jnp.abs(w), axis=0, keepdims=True), F8_TINY)
    q_ref[...] = (w * (F8_MAX / amax)).astype(q_ref.dtype)
    s_ref[...] = amax * (1.0 / F8_MAX)


def _fp8_weights(w, *, tn=512):
    depth, k, n = w.shape
    return pl.pallas_call(
        _fp8_weights_kernel,
        grid=(depth, n // tn),
        in_specs=[pl.BlockSpec((None, k, tn), lambda l, j: (l, 0, j))],
        out_specs=[pl.BlockSpec((None, k, tn), lambda l, j: (l, 0, j)),
                   pl.BlockSpec((None, 1, tn), lambda l, j: (l, 0, j))],
        out_shape=[jax.ShapeDtypeStruct((depth, k, n), F8), jax.ShapeDtypeStruct((depth, 1, n), F32)],
        compiler_params=_cparams("parallel", "parallel"),
        name="fp8_weight_prep",
    )(w)


def _ffn(x2, g, mod_l, w1, w3, s1, s3, w2, layer, seq, final_g=None, *, tm=1024, tf=512):
    t, d = x2.shape
    f = w1.shape[2]
    tiles_per_seq = seq // tm
    final = final_g is not None
    in_specs = [
        pl.BlockSpec((tm, d), lambda i, j: (i, 0)),
        pl.BlockSpec((1, d), lambda i, j: (0, 0)),
        pl.BlockSpec((None, 6, d), lambda i, j: (i // tiles_per_seq, 0, 0)),
        pl.BlockSpec((None, d, tf), lambda i, j: (layer, 0, j)),
        pl.BlockSpec((None, d, tf), lambda i, j: (layer, 0, j)),
        pl.BlockSpec((None, 1, tf), lambda i, j: (layer, 0, j)),
        pl.BlockSpec((None, 1, tf), lambda i, j: (layer, 0, j)),
        pl.BlockSpec((None, tf, d), lambda i, j: (layer, j, 0)),
    ]
    args = [x2, g.reshape(1, d), mod_l, w1, w3, s1, s3, w2]
    if final:
        in_specs.append(pl.BlockSpec((1, d), lambda i, j: (0, 0)))
        args.append(final_g.reshape(1, d))
    return pl.pallas_call(
        functools.partial(_ffn_kernel, final=final),
        grid=(t // tm, f // tf),
        in_specs=in_specs,
        out_specs=pl.BlockSpec((tm, d), lambda i, j: (i, 0)),
        out_shape=jax.ShapeDtypeStruct((t, d), F32),
        scratch_shapes=[pltpu.VMEM((tm, d), F8), pltpu.VMEM((tm, 1), F32)],
        compiler_params=_cparams("parallel", "arbitrary"),
        name="swiglu_ffn_final" if final else "swiglu_ffn",
    )(*args)


def _split3(x):
    hi = x.astype(BF16).astype(F32)
    mid = (x - hi).astype(BF16).astype(F32)
    lo = (x - hi - mid).astype(BF16).astype(F32)
    return hi, mid, lo


def _moba_prep_kernel(slopes_ref, q_ref, k_ref, v_ref, qx_ref, kx_ref, vt_ref, *, nb, blk, grp):
    s, dh = q_ref.shape
    u = slopes_ref[pl.program_id(1)] * (dh ** 0.5)
    kmean = jnp.mean(k_ref[...].astype(F32).reshape(nb, blk, dh), axis=1)
    hi = kmean.astype(BF16)
    lo = (kmean - hi.astype(F32)).astype(BF16)
    q = q_ref[...]
    gate = _dot_nt(hi, q) + _dot_nt(lo, q)
    lanes_per = RANK_QBLOCKS * blk
    pieces = []
    for c in range(s // lanes_per):
        nr = min(nb, RANK_QBLOCKS * (c + 1))
        n_idx = lax.broadcasted_iota(jnp.int32, (nr, lanes_per), 0)
        q_blk = lax.broadcasted_iota(jnp.int32, (nr, lanes_per), 1) // blk + c * RANK_QBLOCKS
        eligible = n_idx < q_blk
        g = jnp.where(eligible, gate[0:nr, c * lanes_per:(c + 1) * lanes_per], NEG_INF)
        rank = jnp.zeros((nr, lanes_per), jnp.int32)
        for m in range(nr):
            gm = g[m:m + 1, :]
            beats = (gm > g) | ((gm == g) & (n_idx > m))
            rank += beats.astype(jnp.int32)
        keep = (eligible & (rank < MOBA_TOPK)) | (n_idx == q_blk)
        piece = jnp.where(keep, 0.0, NEG_INF)
        if nr < nb:
            piece = jnp.concatenate([piece, jnp.full((nb - nr, lanes_per), NEG_INF, F32)], axis=0)
        pieces.append(piece)
    sel_bias = jnp.concatenate(pieces, axis=1) if len(pieces) > 1 else pieces[0]

    off = (lax.broadcasted_iota(jnp.int32, (8, s), 1) % blk).astype(F32) * u
    hi, mid, lo = _split3(off)
    feat = lax.broadcasted_iota(jnp.int32, (8, s), 0)
    pos_rows = jnp.where(feat == 0, hi, jnp.where(feat == 1, mid, jnp.where(feat == 2, lo,
                         jnp.where(feat < 6, 1.0, 0.0))))
    ext_t = jnp.concatenate([sel_bias, pos_rows, jnp.zeros((dh - nb - 8, s), F32)], axis=0)
    chunk = grp * blk
    for ci in range(s // chunk):
        rows = slice(ci * chunk, (ci + 1) * chunk)
        qx_ref[rows, 0:dh] = q_ref[rows, :]
        qx_ref[rows, dh:2 * dh] = ext_t[:, rows].T.astype(qx_ref.dtype)

    col = lax.broadcasted_iota(jnp.int32, (blk, dh), 1)
    khi, kmid, klo = _split3(lax.broadcasted_iota(jnp.int32, (blk, dh), 0).astype(F32) * u)
    tile = jnp.where((col >= nb) & (col < nb + 3), -1.0,
                     jnp.where(col == nb + 3, khi, jnp.where(col == nb + 4, kmid,
                               jnp.where(col == nb + 5, klo, 0.0))))
    kx_ref[:, 0:dh] = k_ref[...]
    for n in range(nb):
        kx_ref[n * blk:(n + 1) * blk, dh:2 * dh] = jnp.where(col == n, 1.0, tile).astype(kx_ref.dtype)

    for n in range(nb):
        v_t = v_ref[n * blk:(n + 1) * blk, :].astype(F32).T.astype(vt_ref.dtype)
        vt_ref[n // grp, :, (n % grp) * blk:(n % grp + 1) * blk] = v_t


def _moba_attn_kernel(slopes_ref, qx_ref, kx_ref, vt_ref, o_ref, m_scr, l_scr, acc_scr, *, blk, grp, scale):
    h = pl.program_id(1)
    qt = pl.program_id(2)
    slope2 = slopes_ref[h] * LOG2E
    c2 = scale * LOG2E
    causal = (lax.broadcasted_iota(jnp.int32, (blk, blk), 0)
              <= lax.broadcasted_iota(jnp.int32, (blk, blk), 1))

    def scores(b, kg):
        return _dot_nt(kg, qx_ref[b * blk:(b + 1) * blk, :])

    def update(b, raw, vt_g, n0, n_sub, first):
        q_blk = qt * grp + b
        subs, shifts = [], []
        m_new = None if first else m_scr[b]
        for gi in range(n_sub):
            r = raw[gi * blk:(gi + 1) * blk, :]
            if first and gi == b:
                r = jnp.where(causal, r, NEG_INF)
            shift = -slope2 * (blk * (q_blk - (n0 + gi))).astype(F32)
            cm = jnp.max(r, axis=0, keepdims=True) * c2 + shift
            m_new = cm if m_new is None else jnp.maximum(m_new, cm)
            subs.append(r)
            shifts.append(shift)
        l = None
        probs = []
        for r, shift in zip(subs, shifts):
            p = jnp.exp2(r * c2 - (m_new - shift))
            ps = jnp.sum(p, axis=0, keepdims=True)
            l = ps if l is None else l + ps
            probs.append(p.astype(BF16))
        pv = _dot(vt_g, jnp.concatenate(probs, axis=0) if n_sub > 1 else probs[0])
        if first:
            acc_scr[b] = pv
        else:
            alpha = jnp.exp2(m_scr[b] - m_new)
            l = alpha * l_scr[b] + l
            acc_scr[b] = alpha * acc_scr[b] + pv
        m_scr[b] = m_new
        l_scr[b] = l


    base = pl.multiple_of(qt * (grp * blk), grp * blk)
    raws = [scores(b, kx_ref[pl.ds(base, (b + 1) * blk), :]) for b in range(grp)]
    for b in range(grp):
        update(b, raws[b], vt_ref[qt, :, 0:(b + 1) * blk], qt * grp, b + 1, True)

    def body(g, carry):
        kg = kx_ref[pl.ds(pl.multiple_of(g * (grp * blk), grp * blk), grp * blk), :]
        vt_g = vt_ref[g]
        raws = [scores(b, kg) for b in range(grp)]
        for b in range(grp):
            update(b, raws[b], vt_g, g * grp, grp, False)
        return carry

    lax.fori_loop(0, qt, body, 0)
    for b in range(grp):
        o_ref[b * blk:(b + 1) * blk, :] = (acc_scr[b] / l_scr[b]).T.astype(o_ref.dtype)


def _moba(proj3, slopes, n_heads):
    bsz, s, _ = proj3.shape
    dh, blk, grp = A_HEAD_DIM, MOBA_BLOCK, MOBA_GROUP
    nb = s // blk
    ng = nb // grp
    tq = grp * blk
    smem = pl.BlockSpec(memory_space=pltpu.SMEM)
    qx, kx, vt = pl.pallas_call(
        functools.partial(_moba_prep_kernel, nb=nb, blk=blk, grp=grp),
        grid=(bsz, n_heads),
        in_specs=[
            smem,
            pl.BlockSpec((None, s, dh), lambda b, h: (b, 0, h)),
            pl.BlockSpec((None, s, dh), lambda b, h: (b, 0, n_heads + h)),
            pl.BlockSpec((None, s, dh), lambda b, h: (b, 0, 2 * n_heads + h)),
        ],
        out_specs=[
            pl.BlockSpec((None, None, s, 2 * dh), lambda b, h: (b, h, 0, 0)),
            pl.BlockSpec((None, None, s, 2 * dh), lambda b, h: (b, h, 0, 0)),
            pl.BlockSpec((None, None, ng, dh, tq), lambda b, h: (b, h, 0, 0, 0)),
        ],
        out_shape=[
            jax.ShapeDtypeStruct((bsz, n_heads, s, 2 * dh), BF16),
            jax.ShapeDtypeStruct((bsz, n_heads, s, 2 * dh), BF16),
            jax.ShapeDtypeStruct((bsz, n_heads, ng, dh, tq), BF16),
        ],
        compiler_params=_cparams("parallel", "parallel"),
        name="moba_select",
    )(slopes, proj3, proj3, proj3)
    return pl.pallas_call(
        functools.partial(_moba_attn_kernel, blk=blk, grp=grp, scale=dh ** -0.5),
        grid=(bsz, n_heads, ng),
        in_specs=[
            smem,
            pl.BlockSpec((None, None, tq, 2 * dh), lambda b, h, i: (b, h, i, 0)),
            pl.BlockSpec((None, None, s, 2 * dh), lambda b, h, i: (b, h, 0, 0)),
            pl.BlockSpec((None, None, ng, dh, tq), lambda b, h, i: (b, h, 0, 0, 0)),
        ],
        out_specs=pl.BlockSpec((None, tq, dh), lambda b, h, i: (b, i, h)),
        out_shape=jax.ShapeDtypeStruct((bsz, s, n_heads * dh), BF16),
        scratch_shapes=[pltpu.VMEM((grp, 1, blk), F32), pltpu.VMEM((grp, 1, blk), F32),
                        pltpu.VMEM((grp, dh, blk), F32)],
        compiler_params=_cparams("parallel", "parallel", "arbitrary"),
        name="moba_attention",
    )(slopes, qx, kx, vt)


def _shift_rows(x, d, fill):
    n, c = x.shape
    if d % HALO_ROWS == 0:
        return jnp.concatenate([jnp.full((d, c), fill, x.dtype), x[:n - d]], axis=0)
    rolled = pltpu.roll(x, d, axis=0)
    rows = lax.broadcasted_iota(jnp.int32, x.shape, 0)
    return jnp.where(rows < d, fill, rolled)


def _rglru_kernel(x_ref, xh_ref, gr_ref, cw_ref, cb_ref, w_ref, ba_ref, bx_ref, lam_ref, o_ref,
                  h_scr, *, ts, n_blocks):
    si = pl.program_id(1)

    @pl.when(si == 0)
    def _():
        h_scr[...] = jnp.zeros_like(h_scr)

    xb = x_ref[...]
    cw = cw_ref[...]
    halo = jnp.where(si == 0, 0.0, xh_ref[...].astype(F32))
    halo_row = lax.broadcasted_iota(jnp.int32, halo.shape, 0)
    delta = (lax.broadcasted_iota(jnp.int32, (ts, ts), 0) - lax.broadcasted_iota(jnp.int32, (ts, ts), 1))
    xc = cw[C_CONV - 1:C_CONV, :] * xb.astype(F32) + cb_ref[...]
    head = jnp.zeros_like(halo)
    for j in range(C_CONV - 1):
        d = C_CONV - 1 - j
        shifted = _dot(jnp.where(delta == d, 1.0, 0.0).astype(BF16), xb)
        xc += cw[j:j + 1, :] * shifted
        head += cw[j:j + 1, :] * jnp.where(halo_row < d, pltpu.roll(halo, d, axis=0), 0.0)
    xc = jnp.concatenate([xc[:HALO_ROWS] + head, xc[HALO_ROWS:]], axis=0)

    bd = xc.shape[1] // n_blocks
    xcb = xc.astype(BF16)
    gates = [_dot(xcb[:, g * bd:(g + 1) * bd], w_ref[g]) for g in range(n_blocks)]
    r = jax.nn.sigmoid(jnp.concatenate([gt[:, :bd] for gt in gates], axis=1) + ba_ref[...])
    i = jax.nn.sigmoid(jnp.concatenate([gt[:, bd:] for gt in gates], axis=1) + bx_ref[...])
    log_a = (-RG_C) * r * jax.nn.softplus(-lam_ref[...])
    a = jnp.exp(log_a)
    z = -jnp.tanh(log_a) * (a * a + 1.0)
    u = jnp.where(z > 0.0, z * lax.rsqrt(z), 0.0) * (i * xc)

    d = 1
    while d < ts:
        u = a * _shift_rows(u, d, 0.0) + u
        a = a * _shift_rows(a, d, 1.0)
        d *= 2
    hs = a * h_scr[...] + u
    h_scr[...] = hs[ts - 1:ts, :]
    o_ref[...] = (hs * jax.nn.gelu(gr_ref[...].astype(F32))).astype(o_ref.dtype)


def _rglru(proj3, conv_w, conv_b, w_cat, ba, bx, lam, *, ts=256):
    bsz, s, _ = proj3.shape
    c = conv_w.shape[1]
    n_blocks, bd, _ = w_cat.shape
    hb = ts // HALO_ROWS
    vec = pl.BlockSpec((1, c), lambda b, i: (0, 0))
    return pl.pallas_call(
        functools.partial(_rglru_kernel, ts=ts, n_blocks=n_blocks),
        grid=(bsz, s // ts),
        in_specs=[
            pl.BlockSpec((None, ts, c), lambda b, i: (b, i, 0)),
            pl.BlockSpec((None, HALO_ROWS, c), lambda b, i: (b, jnp.maximum(i * hb - 1, 0), 0)),
            pl.BlockSpec((None, ts, c), lambda b, i: (b, i, 1)),
            pl.BlockSpec((C_CONV, c), lambda b, i: (0, 0)),
            vec,
            pl.BlockSpec((n_blocks, bd, 2 * bd), lambda b, i: (0, 0, 0)),
            vec, vec, vec,
        ],
        out_specs=pl.BlockSpec((None, ts, c), lambda b, i: (b, i, 0)),
        out_shape=jax.ShapeDtypeStruct((bsz, s, c), BF16),
        scratch_shapes=[pltpu.VMEM((1, c), F32)],
        compiler_params=_cparams("parallel", "arbitrary"),
        name="rglru_scan",
    )(proj3, proj3, proj3, conv_w, conv_b.reshape(1, c), w_cat, ba.reshape(1, c), bx.reshape(1, c),
      lam.reshape(1, c))


def _mlstm_kernel(q_ref, k_ref, v_ref, og_ref, gt_ref, o_ref, c_scr, n_scr, m_scr, *, chunk, nh, dh):
    ci = pl.program_id(1)

    @pl.when(ci == 0)
    def _():
        c_scr[...] = jnp.zeros_like(c_scr)
        n_scr[...] = jnp.zeros_like(n_scr)
        m_scr[...] = jnp.zeros_like(m_scr)

    gates = gt_ref[...]
    bcum = jax.nn.log_sigmoid(gates)
    d = 1
    while d < chunk:
        bcum = bcum + _shift_rows(bcum, d, 0.0)
        d *= 2
    lane = lax.broadcasted_iota(jnp.int32, gates.shape, 1)
    rows_t = jnp.where(lane < nh, gates, bcum).T
    r_idx = lax.broadcasted_iota(jnp.int32, (chunk, chunk), 0)
    c_idx = lax.broadcasted_iota(jnp.int32, (chunk, chunk), 1)
    causal = c_idx <= r_idx
    k_scale = dh ** -0.5

    head_cols = [slice(h * dh, (h + 1) * dh) for h in range(nh)]
    qs = [q_ref[:, cols] for cols in head_cols]
    kss = [k_ref[:, cols] * jnp.asarray(k_scale, BF16) for cols in head_cols]
    vs = [v_ref[:, cols] for cols in head_cols]
    c_prevs = [c_scr[h] for h in range(nh)]
    qk = [_dot_nt(qs[h], kss[h]) for h in range(nh)]
    qc = [_dot(qs[h], c_prevs[h].astype(BF16)) for h in range(nh)]

    stage = []
    for h in range(nh):
        i_col = gates[:, h:h + 1]
        b_col = bcum[:, nh + h:nh + h + 1]
        i_row = rows_t[h:h + 1, :]
        b_row = rows_t[nh + h:nh + h + 1, :]
        m_prev = m_scr[h]

        dmat = jnp.where(causal, b_col - b_row + i_row, NEG_INF)
        inter = b_col + m_prev
        m_t = jnp.maximum(inter, jnp.max(dmat, axis=-1, keepdims=True))
        w_inter = jnp.exp(inter - m_t)
        s_qk = qk[h] * jnp.exp(dmat - m_t)
        n_prev = n_scr[h]
        q_n = jnp.sum(qs[h].astype(F32) * n_prev, axis=-1, keepdims=True)
        den = w_inter * q_n + jnp.sum(s_qk, axis=-1, keepdims=True)

        b_last = b_col[chunk - 1:chunk, :]
        w_s = b_last - b_col + i_col
        m_new = jnp.maximum(b_last + m_prev, jnp.max(w_s, axis=0, keepdims=True))
        decay = jnp.exp(b_last + m_prev - m_new)
        kw = jnp.exp(w_s - m_new) * kss[h].astype(F32)
        n_scr[h] = decay * n_prev + jnp.sum(kw, axis=0, keepdims=True)
        m_scr[h] = m_new
        stage.append((s_qk.astype(BF16), kw.astype(BF16), w_inter, den, m_t, decay))

    sv = [_dot(stage[h][0], vs[h]) for h in range(nh)]
    kv = [_dot_tn(stage[h][1], vs[h]) for h in range(nh)]
    for h in range(nh):
        _, _, w_inter, den, m_t, decay = stage[h]
        hid = (w_inter * qc[h] + sv[h]) / jnp.maximum(jnp.abs(den), jnp.exp(-m_t))
        c_scr[h] = decay * c_prevs[h] + kv[h]
        cols = head_cols[h]
        o_ref[:, cols] = (hid * jax.nn.sigmoid(og_ref[:, cols].astype(F32))).astype(o_ref.dtype)


def _mlstm(proj3, gates3, *, col0, nh, chunk=MLSTM_CHUNK):
    bsz, s, _ = proj3.shape
    dh = 256
    w = nh * dh
    cb = col0 // w
    blk = lambda k: pl.BlockSpec((None, chunk, w), lambda b, i: (b, i, cb + k))
    return pl.pallas_call(
        functools.partial(_mlstm_kernel, chunk=chunk, nh=nh, dh=dh),
        grid=(bsz, s // chunk),
        in_specs=[blk(0), blk(1), blk(2), blk(3),
                  pl.BlockSpec((None, chunk, GATE_LANES), lambda b, i: (b, i, 0))],
        out_specs=pl.BlockSpec((None, chunk, w), lambda b, i: (b, i, 0)),
        out_shape=jax.ShapeDtypeStruct((bsz, s, w), BF16),
        scratch_shapes=[pltpu.VMEM((nh, dh, dh), F32), pltpu.VMEM((nh, 1, dh), F32),
                        pltpu.VMEM((nh, 1, 1), F32)],
        compiler_params=_cparams("parallel", "arbitrary"),
        name="mlstm_chunkwise",
    )(proj3, proj3, proj3, proj3, gates3)


def kernel(x, c, mod_w, mod_b, norm_mix_g, norm_ffn_g, ffn_w1, ffn_w3, ffn_w2, ab_w_in, ab_conv_w, ab_w_out, cd_w_in, cd_gate_b, rg_conv_w, rg_conv_b, rg_wa, rg_ba, rg_wx, rg_bx, rg_lambda, cd_w_out, final_norm_g):
    bsz, seq, d = x.shape
    depth = mod_w.shape[0]
    t = bsz * seq
    half = d // 2
    a_heads = half // A_HEAD_DIM
    slopes = 2.0 ** (-8.0 * jnp.arange(1, a_heads + 1, dtype=F32) / a_heads)

    mod = _modulation(c, mod_w, mod_b)
    ffn_w1_q, ffn_s1 = _fp8_weights(ffn_w1)
    ffn_w3_q, ffn_s3 = _fp8_weights(ffn_w3)
    ffn_w2_b = ffn_w2.astype(BF16)
    ab_w_in_b, ab_w_out_b = ab_w_in.astype(BF16), ab_w_out.astype(BF16)
    cd_w_in_b, cd_w_out_b = cd_w_in.astype(BF16), cd_w_out.astype(BF16)
    x2 = x.reshape(t, d)
    for layer in range(depth):
        mod_l = mod[layer]
        if layer % 2 == 0:
            e = layer // 2
            n_main = 6 * half
            proj = _in_proj(x2, norm_mix_g[layer], mod_l, ab_w_in_b, e, n_main, seq, shift_row=0, scale_row=1)
            proj3 = proj.reshape(bsz, seq, n_main)
            y_a = _moba(proj3, slopes, a_heads).reshape(t, half)
            y_b, conv = proj, dict(conv_w=ab_conv_w[e], conv_col0=3 * half)
            w_out, w_idx = ab_w_out_b, e
        else:
            o = layer // 2
            n_main = 2 * half + 4 * half
            n_g = cd_w_in.shape[2] - n_main
            w_gate = jnp.zeros((d, GATE_LANES), BF16).at[:, :n_g].set(cd_w_in[o, :, n_main:].astype(BF16))
            b_gate = jnp.zeros((1, GATE_LANES), F32).at[0, :n_g].set(cd_gate_b[o])
            proj, gates = _in_proj(x2, norm_mix_g[layer], mod_l, cd_w_in_b, o, n_main, seq,
                                   shift_row=0, scale_row=1, w_gate=w_gate, b_gate=b_gate)
            proj3 = proj.reshape(bsz, seq, n_main)
            w_cat = jnp.concatenate([rg_wa[o], rg_wx[o]], axis=-1).astype(BF16)
            y_a = _rglru(proj3, rg_conv_w[o], rg_conv_b[o], w_cat, rg_ba[o], rg_bx[o],
                         rg_lambda[o]).reshape(t, half)
            y_b = _mlstm(proj3, gates.reshape(bsz, seq, GATE_LANES), col0=2 * half,
                         nh=D_HEADS).reshape(t, half)
            w_out, w_idx, conv = cd_w_out_b, o, {}
        x2 = _out_proj(y_a, y_b, w_out, w_idx, x2, mod_l, seq, gate_row=2, **conv)
        final_g = final_norm_g if layer == depth - 1 else None
        x2 = _ffn(x2, norm_ffn_g[layer], mod_l, ffn_w1_q, ffn_w3_q, ffn_s1, ffn_s3, ffn_w2_b,
                  layer, seq, final_g)
    return x2.reshape(bsz, seq, d)
```

```python
import functools

import jax
import jax.numpy as jnp
from jax import lax
from jax.experimental import pallas as pl
from jax.experimental.pallas import tpu as pltpu

F32 = jnp.float32
BF16 = jnp.bfloat16
F8 = jnp.float8_e4m3fn
F8_MAX = 448.0
F8_TINY = 1e-30

EPS = 1e-6
NEG_INF = -1e30
LOG2E = 1.4426950408889634

A_HEAD_DIM = 128
MOBA_BLOCK = 256
MOBA_TOPK = 3
MOBA_GROUP = 4
RANK_QBLOCKS = 8
B_CONV = 3
C_BLOCKS = 8
C_CONV = 4
RG_C = 8.0
D_HEADS = 4
MLSTM_CHUNK = 256
GATE_LANES = 128

VMEM_LIMIT_BYTES = 60 * 1024 * 1024
HALO_ROWS = 8
NORM_CHUNKS = 4


def _cparams(*sem):
    return pltpu.CompilerParams(dimension_semantics=sem, vmem_limit_bytes=VMEM_LIMIT_BYTES)


def _dot(a, b):
    return jnp.dot(a, b, preferred_element_type=F32)


def _dot_nt(a, b):
    return lax.dot_general(a, b, (((1,), (1,)), ((), ())), preferred_element_type=F32)


def _dot_tn(a, b):
    return lax.dot_general(a, b, (((0,), (0,)), ((), ())), preferred_element_type=F32)


def _norm_mod(x, g, scale, shift):
    xn = x * lax.rsqrt(jnp.mean(x * x, axis=-1, keepdims=True) + EPS)
    return (xn * g) * (1.0 + scale) + shift


def _mod_kernel(c_ref, w_ref, b_ref, o_ref):
    c = c_ref[...]
    cond = c * jax.nn.sigmoid(c)
    o_ref[...] = _dot(cond.astype(BF16), w_ref[...].astype(BF16)) + b_ref[...]


def _modulation(c, mod_w, mod_b):
    depth, d, n = mod_w.shape
    bsz = c.shape[0]
    rows = 8
    c8 = jnp.zeros((rows, d), F32).at[:bsz].set(c)
    tn = 1024
    out = pl.pallas_call(
        _mod_kernel,
        grid=(depth, n // tn),
        in_specs=[
            pl.BlockSpec((rows, d), lambda l, j: (0, 0)),
            pl.BlockSpec((None, d, tn), lambda l, j: (l, 0, j)),
            pl.BlockSpec((None, 1, tn), lambda l, j: (l, 0, j)),
        ],
        out_specs=pl.BlockSpec((None, rows, tn), lambda l, j: (l, 0, j)),
        out_shape=jax.ShapeDtypeStruct((depth, rows, n), F32),
        compiler_params=_cparams("arbitrary", "arbitrary"),
        name="adaln_modulation",
    )(c8, mod_w, mod_b.reshape(depth, 1, n))
    return out[:, :bsz].reshape(depth, bsz, 6, d)


def _in_proj_kernel(x_ref, g_ref, mod_ref, w_ref, *rest, shift_row, scale_row, with_gates):
    if with_gates:
        wg_ref, bg_ref, o_ref, og_ref, h_scr = rest
    else:
        o_ref, h_scr = rest
    j = pl.program_id(1)

    @pl.when(j == 0)
    def _():
        rc = x_ref.shape[0] // NORM_CHUNKS
        for r in range(NORM_CHUNKS):
            rows = slice(r * rc, (r + 1) * rc)
            h = _norm_mod(x_ref[rows, :], g_ref[...], mod_ref[scale_row:scale_row + 1, :],
                          mod_ref[shift_row:shift_row + 1, :])
            hb = h.astype(BF16)
            h_scr[rows, :] = hb
            o_ref[rows, :] = _dot(hb, w_ref[...]).astype(o_ref.dtype)
            if with_gates:
                og_ref[rows, :] = _dot(hb, wg_ref[...]) + bg_ref[...]

    @pl.when(j > 0)
    def _():
        o_ref[...] = _dot(h_scr[...], w_ref[...]).astype(o_ref.dtype)


def _in_proj(x2, g, mod_l, w, idx, n, seq, *, shift_row, scale_row, w_gate=None, b_gate=None, tm=1024, tn=2048):
    t, d = x2.shape
    tiles_per_seq = seq // tm
    with_gates = w_gate is not None
    in_specs = [
        pl.BlockSpec((tm, d), lambda i, j: (i, 0)),
        pl.BlockSpec((1, d), lambda i, j: (0, 0)),
        pl.BlockSpec((None, 6, d), lambda i, j: (i // tiles_per_seq, 0, 0)),
        pl.BlockSpec((None, d, tn), lambda i, j: (idx, 0, j)),
    ]
    args = [x2, g.reshape(1, d), mod_l, w]
    out_specs = pl.BlockSpec((tm, tn), lambda i, j: (i, j))
    out_shape = jax.ShapeDtypeStruct((t, n), BF16)
    if with_gates:
        in_specs += [pl.BlockSpec((d, GATE_LANES), lambda i, j: (0, 0)),
                     pl.BlockSpec((1, GATE_LANES), lambda i, j: (0, 0))]
        args += [w_gate, b_gate]
        out_specs = [out_specs, pl.BlockSpec((tm, GATE_LANES), lambda i, j: (i, 0))]
        out_shape = [out_shape, jax.ShapeDtypeStruct((t, GATE_LANES), F32)]
    return pl.pallas_call(
        functools.partial(_in_proj_kernel, shift_row=shift_row, scale_row=scale_row, with_gates=with_gates),
        grid=(t // tm, n // tn),
        in_specs=in_specs,
        out_specs=out_specs,
        out_shape=out_shape,
        scratch_shapes=[pltpu.VMEM((tm, d), BF16)],
        compiler_params=_cparams("parallel", "arbitrary"),
        name="norm_in_proj_gates" if with_gates else "norm_in_proj",
    )(*args)


def _gated_conv_tile(gb_ref, gc_ref, xb_ref, gch_ref, xbh_ref, w_ref, u_scr, first):
    tm = gc_ref.shape[0]
    u_scr[0:HALO_ROWS, :] = jnp.where(first, 0.0, gch_ref[...].astype(F32) * xbh_ref[...].astype(F32))
    u = gc_ref[...].astype(F32) * xb_ref[...].astype(F32)
    u_scr[HALO_ROWS:, :] = u
    w = w_ref[...]
    conv = w[B_CONV - 1:B_CONV, :] * u
    for j in range(B_CONV - 1):
        off = HALO_ROWS - (B_CONV - 1) + j
        conv += w[j:j + 1, :] * u_scr[off:off + tm, :]
    return gb_ref[...].astype(F32) * conv


def _out_proj_kernel(a_ref, *rest, gate_row, half, conv_tiles_per_seq):
    if conv_tiles_per_seq:
        gb_ref, gc_ref, xb_ref, gch_ref, xbh_ref, cw_ref, w_ref, x_ref, mod_ref, o_ref, u_scr = rest
    else:
        b_ref, w_ref, x_ref, mod_ref, o_ref = rest
    acc = _dot(a_ref[...], w_ref[0:half, :])
    if conv_tiles_per_seq:
        first = (pl.program_id(0) % conv_tiles_per_seq) == 0
        b = _gated_conv_tile(gb_ref, gc_ref, xb_ref, gch_ref, xbh_ref, cw_ref, u_scr, first).astype(BF16)
    else:
        b = b_ref[...]
    acc += _dot(b, w_ref[half:, :])
    o_ref[...] = x_ref[...] + mod_ref[gate_row:gate_row + 1, :] * acc


def _out_proj(a, b, w, idx, x2, mod_l, seq, *, gate_row, conv_w=None, conv_col0=None, tm=512):
    t, d = x2.shape
    half = a.shape[1]
    tiles_per_seq = seq // tm
    fused_conv = conv_w is not None
    row_block = pl.BlockSpec((tm, half), lambda i: (i, 0))
    if fused_conv:
        cb = conv_col0 // half
        hb = tm // HALO_ROWS
        halo = lambda k: pl.BlockSpec((HALO_ROWS, half), lambda i: (jnp.maximum(i * hb - 1, 0), cb + k))
        b_specs = [pl.BlockSpec((tm, half), lambda i: (i, cb)), pl.BlockSpec((tm, half), lambda i: (i, cb + 1)),
                   pl.BlockSpec((tm, half), lambda i: (i, cb + 2)), halo(1), halo(2),
                   pl.BlockSpec((B_CONV, half), lambda i: (0, 0))]
        b_args = [b, b, b, b, b, conv_w]
        scratch = [pltpu.VMEM((tm + HALO_ROWS, half), F32)]
    else:
        b_specs, b_args, scratch = [row_block], [b], []
    return pl.pallas_call(
        functools.partial(_out_proj_kernel, gate_row=gate_row, half=half,
                          conv_tiles_per_seq=tiles_per_seq if fused_conv else 0),
        grid=(t // tm,),
        in_specs=[row_block] + b_specs + [
            pl.BlockSpec((None, 2 * half, d), lambda i: (idx, 0, 0)),
            pl.BlockSpec((tm, d), lambda i: (i, 0)),
            pl.BlockSpec((None, 6, d), lambda i: (i // tiles_per_seq, 0, 0)),
        ],
        out_specs=pl.BlockSpec((tm, d), lambda i: (i, 0)),
        out_shape=jax.ShapeDtypeStruct((t, d), F32),
        scratch_shapes=scratch,
        compiler_params=_cparams("parallel"),
        name="out_proj_conv_residual" if fused_conv else "out_proj_residual",
    )(a, *b_args, w, x2, mod_l)


def _ffn_up_kernel(x_ref, g_ref, mod_ref, w1_ref, w3_ref, s1_ref, s3_ref, o_ref, h_scr, hs_scr):
    j = pl.program_id(1)
    tm = x_ref.shape[0]

    def gated(h8, row_scale):
        a = _dot(h8, w1_ref[...]) * row_scale * s1_ref[...]
        b = _dot(h8, w3_ref[...]) * row_scale * s3_ref[...]
        return ((a * jax.nn.sigmoid(a)) * b).astype(o_ref.dtype)

    @pl.when(j == 0)
    def _():
        rc = tm // NORM_CHUNKS
        for r in range(NORM_CHUNKS):
            rows = slice(r * rc, (r + 1) * rc)
            h = _norm_mod(x_ref[rows, :], g_ref[...], mod_ref[4:5, :], mod_ref[3:4, :])
            amax = jnp.maximum(jnp.max(jnp.abs(h), axis=-1, keepdims=True), F8_TINY)
            h8 = (h * (F8_MAX / amax)).astype(F8)
            row_scale = amax * (1.0 / F8_MAX)
            h_scr[rows, :] = h8
            hs_scr[rows, :] = row_scale
            o_ref[rows, :] = gated(h8, row_scale)

    @pl.when(j > 0)
    def _():
        o_ref[...] = gated(h_scr[...], hs_scr[...])


def _ffn_down_kernel(u_ref, w2_ref, x_ref, mod_ref, *rest, final):
    if final:
        fg_ref, o_ref = rest
    else:
        (o_ref,) = rest
    y = x_ref[...] + mod_ref[5:6, :] * _dot(u_ref[...], w2_ref[...])
    if final:
        y = y * lax.rsqrt(jnp.mean(y * y, axis=-1, keepdims=True) + EPS) * fg_ref[...]
    o_ref[...] = y


def _fp8_weights_kernel(w_ref, q_ref, s_ref):
    w = w_ref[...]
    amax = jnp.maximum(jnp.max(jnp.abs(w), axis=0, keepdims=True), F8_TINY)
    q_ref[...] = (w * (F8_MAX / amax)).astype(q_ref.dtype)
    s_ref[...] = amax * (1.0 / F8_MAX)


def _fp8_weights(w, *, tn=512):
    depth, k, n = w.shape
    return pl.pallas_call(
        _fp8_weights_kernel,
        grid=(depth, n // tn),
        in_specs=[pl.BlockSpec((None, k, tn), lambda l, j: (l, 0, j))],
        out_specs=[pl.BlockSpec((None, k, tn), lambda l, j: (l, 0, j)),
                   pl.BlockSpec((None, 1, tn), lambda l, j: (l, 0, j))],
        out_shape=[jax.ShapeDtypeStruct((depth, k, n), F8), jax.ShapeDtypeStruct((depth, 1, n), F32)],
        compiler_params=_cparams("parallel", "parallel"),
        name="fp8_weight_prep",
    )(w)


def _ffn(x2, g, mod_l, w1, w3, s1, s3, w2, layer, seq, final_g=None, *, tm=1024, tf=512, tm_down=512):
    t, d = x2.shape
    f = w1.shape[2]
    tiles_per_seq = seq // tm
    hidden = pl.pallas_call(
        _ffn_up_kernel,
        grid=(t // tm, f // tf),
        in_specs=[
            pl.BlockSpec((tm, d), lambda i, j: (i, 0)),
            pl.BlockSpec((1, d), lambda i, j: (0, 0)),
            pl.BlockSpec((None, 6, d), lambda i, j: (i // tiles_per_seq, 0, 0)),
            pl.BlockSpec((None, d, tf), lambda i, j: (layer, 0, j)),
            pl.BlockSpec((None, d, tf), lambda i, j: (layer, 0, j)),
            pl.BlockSpec((None, 1, tf), lambda i, j: (layer, 0, j)),
            pl.BlockSpec((None, 1, tf), lambda i, j: (layer, 0, j)),
        ],
        out_specs=pl.BlockSpec((tm, tf), lambda i, j: (i, j)),
        out_shape=jax.ShapeDtypeStruct((t, f), BF16),
        scratch_shapes=[pltpu.VMEM((tm, d), F8), pltpu.VMEM((tm, 1), F32)],
        compiler_params=_cparams("parallel", "arbitrary"),
        name="swiglu_up",
    )(x2, g.reshape(1, d), mod_l, w1, w3, s1, s3)

    final = final_g is not None
    down_tiles_per_seq = seq // tm_down
    in_specs = [
        pl.BlockSpec((tm_down, f), lambda i: (i, 0)),
        pl.BlockSpec((None, f, d), lambda i: (layer, 0, 0), pipeline_mode=pl.Buffered(1)),
        pl.BlockSpec((tm_down, d), lambda i: (i, 0)),
        pl.BlockSpec((None, 6, d), lambda i: (i // down_tiles_per_seq, 0, 0)),
    ]
    args = [hidden, w2, x2, mod_l]
    if final:
        in_specs.append(pl.BlockSpec((1, d), lambda i: (0, 0)))
        args.append(final_g.reshape(1, d))
    return pl.pallas_call(
        functools.partial(_ffn_down_kernel, final=final),
        grid=(t // tm_down,),
        in_specs=in_specs,
        out_specs=pl.BlockSpec((tm_down, d), lambda i: (i, 0)),
        out_shape=jax.ShapeDtypeStruct((t, d), F32),
        compiler_params=_cparams("parallel"),
        name="swiglu_down_final" if final else "swiglu_down",
    )(*args)


def _split3(x):
    hi = x.astype(BF16).astype(F32)
    mid = (x - hi).astype(BF16).astype(F32)
    lo = (x - hi - mid).astype(BF16).astype(F32)
    return hi, mid, lo


def _moba_prep_kernel(slopes_ref, q_ref, k_ref, v_ref, qx_ref, kx_ref, vt_ref, *, nb, blk, grp):
    s, dh = q_ref.shape
    u = slopes_ref[pl.program_id(1)] * (dh ** 0.5)
    kmean = jnp.mean(k_ref[...].astype(F32).reshape(nb, blk, dh), axis=1)
    hi = kmean.astype(BF16)
    lo = (kmean - hi.astype(F32)).astype(BF16)
    q = q_ref[...]
    gate = _dot_nt(hi, q) + _dot_nt(lo, q)
    lanes_per = RANK_QBLOCKS * blk
    pieces = []
    for c in range(s // lanes_per):
        nr = min(nb, RANK_QBLOCKS * (c + 1))
        n_idx = lax.broadcasted_iota(jnp.int32, (nr, lanes_per), 0)
        q_blk = lax.broadcasted_iota(jnp.int32, (nr, lanes_per), 1) // blk + c * RANK_QBLOCKS
        eligible = n_idx < q_blk
        g = jnp.where(eligible, gate[0:nr, c * lanes_per:(c + 1) * lanes_per], NEG_INF)
        rank = jnp.zeros((nr, lanes_per), jnp.int32)
        for m in range(nr):
            gm = g[m:m + 1, :]
            beats = (gm > g) | ((gm == g) & (n_idx > m))
            rank += beats.astype(jnp.int32)
        keep = (eligible & (rank < MOBA_TOPK)) | (n_idx == q_blk)
        piece = jnp.where(keep, 0.0, NEG_INF)
        if nr < nb:
            piece = jnp.concatenate([piece, jnp.full((nb - nr, lanes_per), NEG_INF, F32)], axis=0)
        pieces.append(piece)
    sel_bias = jnp.concatenate(pieces, axis=1) if len(pieces) > 1 else pieces[0]

    off = (lax.broadcasted_iota(jnp.int32, (8, s), 1) % blk).astype(F32) * u
    hi, mid, lo = _split3(off)
    feat = lax.broadcasted_iota(jnp.int32, (8, s), 0)
    pos_rows = jnp.where(feat == 0, hi, jnp.where(feat == 1, mid, jnp.where(feat == 2, lo,
                         jnp.where(feat < 6, 1.0, 0.0))))
    ext_t = jnp.concatenate([sel_bias, pos_rows, jnp.zeros((dh - nb - 8, s), F32)], axis=0)
    chunk = grp * blk
    for ci in range(s // chunk):
        rows = slice(ci * chunk, (ci + 1) * chunk)
        qx_ref[rows, 0:dh] = q_ref[rows, :]
        qx_ref[rows, dh:2 * dh] = ext_t[:, rows].T.astype(qx_ref.dtype)

    col = lax.broadcasted_iota(jnp.int32, (blk, dh), 1)
    khi, kmid, klo = _split3(lax.broadcasted_iota(jnp.int32, (blk, dh), 0).astype(F32) * u)
    tile = jnp.where((col >= nb) & (col < nb + 3), -1.0,
                     jnp.where(col == nb + 3, khi, jnp.where(col == nb + 4, kmid,
                               jnp.where(col == nb + 5, klo, 0.0))))
    kx_ref[:, 0:dh] = k_ref[...]
    for n in range(nb):
        kx_ref[n * blk:(n + 1) * blk, dh:2 * dh] = jnp.where(col == n, 1.0, tile).astype(kx_ref.dtype)

    for n in range(nb):
        v_t = v_ref[n * blk:(n + 1) * blk, :].astype(F32).T.astype(vt_ref.dtype)
        vt_ref[n // grp, :, (n % grp) * blk:(n % grp + 1) * blk] = v_t


def _moba_attn_kernel(slopes_ref, qx_ref, kx_ref, vt_ref, o_ref, m_scr, l_scr, acc_scr, *, blk, grp, scale):
    h = pl.program_id(1)
    qt = pl.program_id(2)
    slope2 = slopes_ref[h] * LOG2E
    c2 = scale * LOG2E
    causal = (lax.broadcasted_iota(jnp.int32, (blk, blk), 0)
              <= lax.broadcasted_iota(jnp.int32, (blk, blk), 1))

    def scores(b, kg):
        return _dot_nt(kg, qx_ref[b * blk:(b + 1) * blk, :])

    def update(b, raw, vt_g, n0, n_sub, first):
        q_blk = qt * grp + b
        subs, shifts = [], []
        m_new = None if first else m_scr[b]
        for gi in range(n_sub):
            r = raw[gi * blk:(gi + 1) * blk, :]
            if first and gi == b:
                r = jnp.where(causal, r, NEG_INF)
            shift = -slope2 * (blk * (q_blk - (n0 + gi))).astype(F32)
            cm = jnp.max(r, axis=0, keepdims=True) * c2 + shift
            m_new = cm if m_new is None else jnp.maximum(m_new, cm)
            subs.append(r)
            shifts.append(shift)
        l = None
        probs = []
        for r, shift in zip(subs, shifts):
            p = jnp.exp2(r * c2 - (m_new - shift))
            ps = jnp.sum(p, axis=0, keepdims=True)
            l = ps if l is None else l + ps
            probs.append(p.astype(BF16))
        pv = _dot(vt_g, jnp.concatenate(probs, axis=0) if n_sub > 1 else probs[0])
        if first:
            acc_scr[b] = pv
        else:
            alpha = jnp.exp2(m_scr[b] - m_new)
            l = alpha * l_scr[b] + l
            acc_scr[b] = alpha * acc_scr[b] + pv
        m_scr[b] = m_new
        l_scr[b] = l


    base = pl.multiple_of(qt * (grp * blk), grp * blk)
    raws = [scores(b, kx_ref[pl.ds(base, (b + 1) * blk), :]) for b in range(grp)]
    for b in range(grp):
        update(b, raws[b], vt_ref[qt, :, 0:(b + 1) * blk], qt * grp, b + 1, True)

    def body(g, carry):
        kg = kx_ref[pl.ds(pl.multiple_of(g * (grp * blk), grp * blk), grp * blk), :]
        vt_g = vt_ref[g]
        raws = [scores(b, kg) for b in range(grp)]
        for b in range(grp):
            update(b, raws[b], vt_g, g * grp, grp, False)
        return carry

    lax.fori_loop(0, qt, body, 0)
    for b in range(grp):
        o_ref[b * blk:(b + 1) * blk, :] = (acc_scr[b] / l_scr[b]).T.astype(o_ref.dtype)


def _moba(proj3, slopes, n_heads):
    bsz, s, _ = proj3.shape
    dh, blk, grp = A_HEAD_DIM, MOBA_BLOCK, MOBA_GROUP
    nb = s // blk
    ng = nb // grp
    tq = grp * blk
    smem = pl.BlockSpec(memory_space=pltpu.SMEM)
    qx, kx, vt = pl.pallas_call(
        functools.partial(_moba_prep_kernel, nb=nb, blk=blk, grp=grp),
        grid=(bsz, n_heads),
        in_specs=[
            smem,
            pl.BlockSpec((None, s, dh), lambda b, h: (b, 0, h)),
            pl.BlockSpec((None, s, dh), lambda b, h: (b, 0, n_heads + h)),
            pl.BlockSpec((None, s, dh), lambda b, h: (b, 0, 2 * n_heads + h)),
        ],
        out_specs=[
            pl.BlockSpec((None, None, s, 2 * dh), lambda b, h: (b, h, 0, 0)),
            pl.BlockSpec((None, None, s, 2 * dh), lambda b, h: (b, h, 0, 0)),
            pl.BlockSpec((None, None, ng, dh, tq), lambda b, h: (b, h, 0, 0, 0)),
        ],
        out_shape=[
            jax.ShapeDtypeStruct((bsz, n_heads, s, 2 * dh), BF16),
            jax.ShapeDtypeStruct((bsz, n_heads, s, 2 * dh), BF16),
            jax.ShapeDtypeStruct((bsz, n_heads, ng, dh, tq), BF16),
        ],
        compiler_params=_cparams("parallel", "parallel"),
        name="moba_select",
    )(slopes, proj3, proj3, proj3)
    return pl.pallas_call(
        functools.partial(_moba_attn_kernel, blk=blk, grp=grp, scale=dh ** -0.5),
        grid=(bsz, n_heads, ng),
        in_specs=[
            smem,
            pl.BlockSpec((None, None, tq, 2 * dh), lambda b, h, i: (b, h, i, 0)),
            pl.BlockSpec((None, None, s, 2 * dh), lambda b, h, i: (b, h, 0, 0)),
            pl.BlockSpec((None, None, ng, dh, tq), lambda b, h, i: (b, h, 0, 0, 0)),
        ],
        out_specs=pl.BlockSpec((None, tq, dh), lambda b, h, i: (b, i, h)),
        out_shape=jax.ShapeDtypeStruct((bsz, s, n_heads * dh), BF16),
        scratch_shapes=[pltpu.VMEM((grp, 1, blk), F32), pltpu.VMEM((grp, 1, blk), F32),
                        pltpu.VMEM((grp, dh, blk), F32)],
        compiler_params=_cparams("parallel", "parallel", "arbitrary"),
        name="moba_attention",
    )(slopes, qx, kx, vt)


def _shift_rows(x, d, fill):
    n, c = x.shape
    if d % HALO_ROWS == 0:
        return jnp.concatenate([jnp.full((d, c), fill, x.dtype), x[:n - d]], axis=0)
    rolled = pltpu.roll(x, d, axis=0)
    rows = lax.broadcasted_iota(jnp.int32, x.shape, 0)
    return jnp.where(rows < d, fill, rolled)


def _rglru_kernel(x_ref, xh_ref, gr_ref, cw_ref, cb_ref, w_ref, ba_ref, bx_ref, lam_ref, o_ref,
                  h_scr, *, ts, n_blocks):
    si = pl.program_id(1)

    @pl.when(si == 0)
    def _():
        h_scr[...] = jnp.zeros_like(h_scr)

    xb = x_ref[...]
    cw = cw_ref[...]
    halo = jnp.where(si == 0, 0.0, xh_ref[...].astype(F32))
    halo_row = lax.broadcasted_iota(jnp.int32, halo.shape, 0)
    delta = (lax.broadcasted_iota(jnp.int32, (ts, ts), 0) - lax.broadcasted_iota(jnp.int32, (ts, ts), 1))
    xc = cw[C_CONV - 1:C_CONV, :] * xb.astype(F32) + cb_ref[...]
    head = jnp.zeros_like(halo)
    for j in range(C_CONV - 1):
        d = C_CONV - 1 - j
        shifted = _dot(jnp.where(delta == d, 1.0, 0.0).astype(BF16), xb)
        xc += cw[j:j + 1, :] * shifted
        head += cw[j:j + 1, :] * jnp.where(halo_row < d, pltpu.roll(halo, d, axis=0), 0.0)
    xc = jnp.concatenate([xc[:HALO_ROWS] + head, xc[HALO_ROWS:]], axis=0)

    bd = xc.shape[1] // n_blocks
    xcb = xc.astype(BF16)
    gates = [_dot(xcb[:, g * bd:(g + 1) * bd], w_ref[g]) for g in range(n_blocks)]
    r = jax.nn.sigmoid(jnp.concatenate([gt[:, :bd] for gt in gates], axis=1) + ba_ref[...])
    i = jax.nn.sigmoid(jnp.concatenate([gt[:, bd:] for gt in gates], axis=1) + bx_ref[...])
    log_a = (-RG_C) * r * jax.nn.softplus(-lam_ref[...])
    a = jnp.exp(log_a)
    z = -jnp.tanh(log_a) * (a * a + 1.0)
    u = jnp.where(z > 0.0, z * lax.rsqrt(z), 0.0) * (i * xc)

    d = 1
    while d < ts:
        u = a * _shift_rows(u, d, 0.0) + u
        a = a * _shift_rows(a, d, 1.0)
        d *= 2
    hs = a * h_scr[...] + u
    h_scr[...] = hs[ts - 1:ts, :]
    o_ref[...] = (hs * jax.nn.gelu(gr_ref[...].astype(F32))).astype(o_ref.dtype)


def _rglru(proj3, conv_w, conv_b, w_cat, ba, bx, lam, *, ts=256):
    bsz, s, _ = proj3.shape
    c = conv_w.shape[1]
    n_blocks, bd, _ = w_cat.shape
    hb = ts // HALO_ROWS
    vec = pl.BlockSpec((1, c), lambda b, i: (0, 0))
    return pl.pallas_call(
        functools.partial(_rglru_kernel, ts=ts, n_blocks=n_blocks),
        grid=(bsz, s // ts),
        in_specs=[
            pl.BlockSpec((None, ts, c), lambda b, i: (b, i, 0)),
            pl.BlockSpec((None, HALO_ROWS, c), lambda b, i: (b, jnp.maximum(i * hb - 1, 0), 0)),
            pl.BlockSpec((None, ts, c), lambda b, i: (b, i, 1)),
            pl.BlockSpec((C_CONV, c), lambda b, i: (0, 0)),
            vec,
            pl.BlockSpec((n_blocks, bd, 2 * bd), lambda b, i: (0, 0, 0)),
            vec, vec, vec,
        ],
        out_specs=pl.BlockSpec((None, ts, c), lambda b, i: (b, i, 0)),
        out_shape=jax.ShapeDtypeStruct((bsz, s, c), BF16),
        scratch_shapes=[pltpu.VMEM((1, c), F32)],
        compiler_params=_cparams("parallel", "arbitrary"),
        name="rglru_scan",
    )(proj3, proj3, proj3, conv_w, conv_b.reshape(1, c), w_cat, ba.reshape(1, c), bx.reshape(1, c),
      lam.reshape(1, c))


def _mlstm_kernel(q_ref, k_ref, v_ref, og_ref, gt_ref, o_ref, c_scr, n_scr, m_scr, *, chunk, nh, dh):
    ci = pl.program_id(1)

    @pl.when(ci == 0)
    def _():
        c_scr[...] = jnp.zeros_like(c_scr)
        n_scr[...] = jnp.zeros_like(n_scr)
        m_scr[...] = jnp.zeros_like(m_scr)

    gates = gt_ref[...]
    bcum = jax.nn.log_sigmoid(gates)
    d = 1
    while d < chunk:
        bcum = bcum + _shift_rows(bcum, d, 0.0)
        d *= 2
    lane = lax.broadcasted_iota(jnp.int32, gates.shape, 1)
    rows_t = jnp.where(lane < nh, gates, bcum).T
    r_idx = lax.broadcasted_iota(jnp.int32, (chunk, chunk), 0)
    c_idx = lax.broadcasted_iota(jnp.int32, (chunk, chunk), 1)
    causal = c_idx <= r_idx
    k_scale = dh ** -0.5

    head_cols = [slice(h * dh, (h + 1) * dh) for h in range(nh)]
    qs = [q_ref[:, cols] for cols in head_cols]
    kss = [k_ref[:, cols] * jnp.asarray(k_scale, BF16) for cols in head_cols]
    vs = [v_ref[:, cols] for cols in head_cols]
    c_prevs = [c_scr[h] for h in range(nh)]
    qk = [_dot_nt(qs[h], kss[h]) for h in range(nh)]
    qc = [_dot(qs[h], c_prevs[h].astype(BF16)) for h in range(nh)]

    stage = []
    for h in range(nh):
        i_col = gates[:, h:h + 1]
        b_col = bcum[:, nh + h:nh + h + 1]
        i_row = rows_t[h:h + 1, :]
        b_row = rows_t[nh + h:nh + h + 1, :]
        m_prev = m_scr[h]

        dmat = jnp.where(causal, b_col - b_row + i_row, NEG_INF)
        inter = b_col + m_prev
        m_t = jnp.maximum(inter, jnp.max(dmat, axis=-1, keepdims=True))
        w_inter = jnp.exp(inter - m_t)
        s_qk = qk[h] * jnp.exp(dmat - m_t)
        n_prev = n_scr[h]
        q_n = jnp.sum(qs[h].astype(F32) * n_prev, axis=-1, keepdims=True)
        den = w_inter * q_n + jnp.sum(s_qk, axis=-1, keepdims=True)

        b_last = b_col[chunk - 1:chunk, :]
        w_s = b_last - b_col + i_col
        m_new = jnp.maximum(b_last + m_prev, jnp.max(w_s, axis=0, keepdims=True))
        decay = jnp.exp(b_last + m_prev - m_new)
        kw = jnp.exp(w_s - m_new) * kss[h].astype(F32)
        n_scr[h] = decay * n_prev + jnp.sum(kw, axis=0, keepdims=True)
        m_scr[h] = m_new
        stage.append((s_qk.astype(BF16), kw.astype(BF16), w_inter, den, m_t, decay))

    sv = [_dot(stage[h][0], vs[h]) for h in range(nh)]
    kv = [_dot_tn(stage[h][1], vs[h]) for h in range(nh)]
    for h in range(nh):
        _, _, w_inter, den, m_t, decay = stage[h]
        hid = (w_inter * qc[h] + sv[h]) / jnp.maximum(jnp.abs(den), jnp.exp(-m_t))
        c_scr[h] = decay * c_prevs[h] + kv[h]
        cols = head_cols[h]
        o_ref[:, cols] = (hid * jax.nn.sigmoid(og_ref[:, cols].astype(F32))).astype(o_ref.dtype)


def _mlstm(proj3, gates3, *, col0, nh, chunk=MLSTM_CHUNK):
    bsz, s, _ = proj3.shape
    dh = 256
    w = nh * dh
    cb = col0 // w
    blk = lambda k: pl.BlockSpec((None, chunk, w), lambda b, i: (b, i, cb + k))
    return pl.pallas_call(
        functools.partial(_mlstm_kernel, chunk=chunk, nh=nh, dh=dh),
        grid=(bsz, s // chunk),
        in_specs=[blk(0), blk(1), blk(2), blk(3),
                  pl.BlockSpec((None, chunk, GATE_LANES), lambda b, i: (b, i, 0))],
        out_specs=pl.BlockSpec((None, chunk, w), lambda b, i: (b, i, 0)),
        out_shape=jax.ShapeDtypeStruct((bsz, s, w), BF16),
        scratch_shapes=[pltpu.VMEM((nh, dh, dh), F32), pltpu.VMEM((nh, 1, dh), F32),
                        pltpu.VMEM((nh, 1, 1), F32)],
        compiler_params=_cparams("parallel", "arbitrary"),
        name="mlstm_chunkwise",
    )(proj3, proj3, proj3, proj3, gates3)


def kernel(x, c, mod_w, mod_b, norm_mix_g, norm_ffn_g, ffn_w1, ffn_w3, ffn_w2, ab_w_in, ab_conv_w, ab_w_out, cd_w_in, cd_gate_b, rg_conv_w, rg_conv_b, rg_wa, rg_ba, rg_wx, rg_bx, rg_lambda, cd_w_out, final_norm_g):
    bsz, seq, d = x.shape
    depth = mod_w.shape[0]
    t = bsz * seq
    half = d // 2
    a_heads = half // A_HEAD_DIM
    slopes = 2.0 ** (-8.0 * jnp.arange(1, a_heads + 1, dtype=F32) / a_heads)

    mod = _modulation(c, mod_w, mod_b)
    ffn_w1_q, ffn_s1 = _fp8_weights(ffn_w1)
    ffn_w3_q, ffn_s3 = _fp8_weights(ffn_w3)
    ffn_w2_b = ffn_w2.astype(BF16)
    ab_w_in_b, ab_w_out_b = ab_w_in.astype(BF16), ab_w_out.astype(BF16)
    cd_w_in_b, cd_w_out_b = cd_w_in.astype(BF16), cd_w_out.astype(BF16)
    x2 = x.reshape(t, d)
    for layer in range(depth):
        mod_l = mod[layer]
        if layer % 2 == 0:
            e = layer // 2
            n_main = 6 * half
            proj = _in_proj(x2, norm_mix_g[layer], mod_l, ab_w_in_b, e, n_main, seq, shift_row=0, scale_row=1)
            proj3 = proj.reshape(bsz, seq, n_main)
            y_a = _moba(proj3, slopes, a_heads).reshape(t, half)
            y_b, conv = proj, dict(conv_w=ab_conv_w[e], conv_col0=3 * half)
            w_out, w_idx = ab_w_out_b, e
        else:
            o = layer // 2
            n_main = 2 * half + 4 * half
            n_g = cd_w_in.shape[2] - n_main
            w_gate = jnp.zeros((d, GATE_LANES), BF16).at[:, :n_g].set(cd_w_in[o, :, n_main:].astype(BF16))
            b_gate = jnp.zeros((1, GATE_LANES), F32).at[0, :n_g].set(cd_gate_b[o])
            proj, gates = _in_proj(x2, norm_mix_g[layer], mod_l, cd_w_in_b, o, n_main, seq,
                                   shift_row=0, scale_row=1, w_gate=w_gate, b_gate=b_gate)
            proj3 = proj.reshape(bsz, seq, n_main)
            w_cat = jnp.concatenate([rg_wa[o], rg_wx[o]], axis=-1).astype(BF16)
            y_a = _rglru(proj3, rg_conv_w[o], rg_conv_b[o], w_cat, rg_ba[o], rg_bx[o],
                         rg_lambda[o]).reshape(t, half)
            y_b = _mlstm(proj3, gates.reshape(bsz, seq, GATE_LANES), col0=2 * half,
                         nh=D_HEADS).reshape(t, half)
            w_out, w_idx, conv = cd_w_out_b, o, {}
        x2 = _out_proj(y_a, y_b, w_out, w_idx, x2, mod_l, seq, gate_row=2, **conv)
        final_g = final_norm_g if layer == depth - 1 else None
        x2 = _ffn(x2, norm_ffn_g[layer], mod_l, ffn_w1_q, ffn_w3_q, ffn_s1, ffn_s3, ffn_w2_b,
                  layer, seq, final_g)
    return x2.reshape(bsz, seq, d)
```

```python
import functools

import jax
import jax.numpy as jnp
from jax import lax
from jax.experimental import pallas as pl
from jax.experimental.pallas import tpu as pltpu

F32 = jnp.float32
BF16 = jnp.bfloat16
F8 = jnp.float8_e4m3fn
F8_MAX = 448.0
F8_TINY = 1e-30

EPS = 1e-6
NEG_INF = -1e30
LOG2E = 1.4426950408889634

A_HEAD_DIM = 128
MOBA_BLOCK = 256
MOBA_TOPK = 3
MOBA_GROUP = 4
RANK_QBLOCKS = 8
B_CONV = 3
C_BLOCKS = 8
C_CONV = 4
RG_C = 8.0
D_HEADS = 4
MLSTM_CHUNK = 256
GATE_LANES = 128

VMEM_LIMIT_BYTES = 60 * 1024 * 1024
HALO_ROWS = 8
NORM_CHUNKS = 4


def _cparams(*sem):
    return pltpu.CompilerParams(dimension_semantics=sem, vmem_limit_bytes=VMEM_LIMIT_BYTES)


def _dot(a, b):
    return jnp.dot(a, b, preferred_element_type=F32)


def _dot_nt(a, b):
    return lax.dot_general(a, b, (((1,), (1,)), ((), ())), preferred_element_type=F32)


def _dot_tn(a, b):
    return lax.dot_general(a, b, (((0,), (0,)), ((), ())), preferred_element_type=F32)


def _norm_mod(x, g, scale, shift):
    xn = x * lax.rsqrt(jnp.mean(x * x, axis=-1, keepdims=True) + EPS)
    return (xn * g) * (1.0 + scale) + shift


def _mod_kernel(c_ref, w_ref, b_ref, o_ref):
    c = c_ref[...]
    cond = c * jax.nn.sigmoid(c)
    o_ref[...] = _dot(cond.astype(BF16), w_ref[...].astype(BF16)) + b_ref[...]


def _modulation(c, mod_w, mod_b):
    depth, d, n = mod_w.shape
    bsz = c.shape[0]
    rows = 8
    c8 = jnp.zeros((rows, d), F32).at[:bsz].set(c)
    tn = 1024
    out = pl.pallas_call(
        _mod_kernel,
        grid=(depth, n // tn),
        in_specs=[
            pl.BlockSpec((rows, d), lambda l, j: (0, 0)),
            pl.BlockSpec((None, d, tn), lambda l, j: (l, 0, j)),
            pl.BlockSpec((None, 1, tn), lambda l, j: (l, 0, j)),
        ],
        out_specs=pl.BlockSpec((None, rows, tn), lambda l, j: (l, 0, j)),
        out_shape=jax.ShapeDtypeStruct((depth, rows, n), F32),
        compiler_params=_cparams("arbitrary", "arbitrary"),
        name="adaln_modulation",
    )(c8, mod_w, mod_b.reshape(depth, 1, n))
    return out[:, :bsz].reshape(depth, bsz, 6, d)


def _in_proj_kernel(x_ref, g_ref, mod_ref, w_ref, *rest, shift_row, scale_row, with_gates):
    if with_gates:
        wg_ref, bg_ref, o_ref, og_ref, h_scr = rest
    else:
        o_ref, h_scr = rest
    j = pl.program_id(1)

    @pl.when(j == 0)
    def _():
        rc = x_ref.shape[0] // NORM_CHUNKS
        for r in range(NORM_CHUNKS):
            rows = slice(r * rc, (r + 1) * rc)
            h = _norm_mod(x_ref[rows, :], g_ref[...], mod_ref[scale_row:scale_row + 1, :],
                          mod_ref[shift_row:shift_row + 1, :])
            hb = h.astype(BF16)
            h_scr[rows, :] = hb
            o_ref[rows, :] = _dot(hb, w_ref[...]).astype(o_ref.dtype)
            if with_gates:
                og_ref[rows, :] = _dot(hb, wg_ref[...]) + bg_ref[...]

    @pl.when(j > 0)
    def _():
        o_ref[...] = _dot(h_scr[...], w_ref[...]).astype(o_ref.dtype)


def _in_proj(x2, g, mod_l, w, idx, n, seq, *, shift_row, scale_row, w_gate=None, b_gate=None, tm=1024, tn=2048):
    t, d = x2.shape
    tiles_per_seq = seq // tm
    with_gates = w_gate is not None
    in_specs = [
        pl.BlockSpec((tm, d), lambda i, j: (i, 0)),
        pl.BlockSpec((1, d), lambda i, j: (0, 0)),
        pl.BlockSpec((None, 6, d), lambda i, j: (i // tiles_per_seq, 0, 0)),
        pl.BlockSpec((None, d, tn), lambda i, j: (idx, 0, j)),
    ]
    args = [x2, g.reshape(1, d), mod_l, w]
    out_specs = pl.BlockSpec((tm, tn), lambda i, j: (i, j))
    out_shape = jax.ShapeDtypeStruct((t, n), BF16)
    if with_gates:
        in_specs += [pl.BlockSpec((d, GATE_LANES), lambda i, j: (0, 0)),
                     pl.BlockSpec((1, GATE_LANES), lambda i, j: (0, 0))]
        args += [w_gate, b_gate]
        out_specs = [out_specs, pl.BlockSpec((tm, GATE_LANES), lambda i, j: (i, 0))]
        out_shape = [out_shape, jax.ShapeDtypeStruct((t, GATE_LANES), F32)]
    return pl.pallas_call(
        functools.partial(_in_proj_kernel, shift_row=shift_row, scale_row=scale_row, with_gates=with_gates),
        grid=(t // tm, n // tn),
        in_specs=in_specs,
        out_specs=out_specs,
        out_shape=out_shape,
        scratch_shapes=[pltpu.VMEM((tm, d), BF16)],
        compiler_params=_cparams("parallel", "arbitrary"),
        name="norm_in_proj_gates" if with_gates else "norm_in_proj",
    )(*args)


def _gated_conv_tile(gb_ref, gc_ref, xb_ref, gch_ref, xbh_ref, w_ref, u_scr, first):
    tm = gc_ref.shape[0]
    u_scr[0:HALO_ROWS, :] = jnp.where(first, 0.0, gch_ref[...].astype(F32) * xbh_ref[...].astype(F32))
    u = gc_ref[...].astype(F32) * xb_ref[...].astype(F32)
    u_scr[HALO_ROWS:, :] = u
    w = w_ref[...]
    conv = w[B_CONV - 1:B_CONV, :] * u
    for j in range(B_CONV - 1):
        off = HALO_ROWS - (B_CONV - 1) + j
        conv += w[j:j + 1, :] * u_scr[off:off + tm, :]
    return gb_ref[...].astype(F32) * conv


def _out_proj_kernel(a_ref, *rest, gate_row, half, conv_tiles_per_seq):
    if conv_tiles_per_seq:
        gb_ref, gc_ref, xb_ref, gch_ref, xbh_ref, cw_ref, w_ref, x_ref, mod_ref, o_ref, u_scr = rest
    else:
        b_ref, w_ref, x_ref, mod_ref, o_ref = rest
    acc = _dot(a_ref[...], w_ref[0:half, :])
    if conv_tiles_per_seq:
        first = (pl.program_id(0) % conv_tiles_per_seq) == 0
        b = _gated_conv_tile(gb_ref, gc_ref, xb_ref, gch_ref, xbh_ref, cw_ref, u_scr, first).astype(BF16)
    else:
        b = b_ref[...]
    acc += _dot(b, w_ref[half:, :])
    o_ref[...] = x_ref[...] + mod_ref[gate_row:gate_row + 1, :] * acc


def _out_proj(a, b, w, idx, x2, mod_l, seq, *, gate_row, conv_w=None, conv_col0=None, tm=512):
    t, d = x2.shape
    half = a.shape[1]
    tiles_per_seq = seq // tm
    fused_conv = conv_w is not None
    row_block = pl.BlockSpec((tm, half), lambda i: (i, 0))
    if fused_conv:
        cb = conv_col0 // half
        hb = tm // HALO_ROWS
        halo = lambda k: pl.BlockSpec((HALO_ROWS, half), lambda i: (jnp.maximum(i * hb - 1, 0), cb + k))
        b_specs = [pl.BlockSpec((tm, half), lambda i: (i, cb)), pl.BlockSpec((tm, half), lambda i: (i, cb + 1)),
                   pl.BlockSpec((tm, half), lambda i: (i, cb + 2)), halo(1), halo(2),
                   pl.BlockSpec((B_CONV, half), lambda i: (0, 0))]
        b_args = [b, b, b, b, b, conv_w]
        scratch = [pltpu.VMEM((tm + HALO_ROWS, half), F32)]
    else:
        b_specs, b_args, scratch = [row_block], [b], []
    return pl.pallas_call(
        functools.partial(_out_proj_kernel, gate_row=gate_row, half=half,
                          conv_tiles_per_seq=tiles_per_seq if fused_conv else 0),
        grid=(t // tm,),
        in_specs=[row_block] + b_specs + [
            pl.BlockSpec((None, 2 * half, d), lambda i: (idx, 0, 0)),
            pl.BlockSpec((tm, d), lambda i: (i, 0)),
            pl.BlockSpec((None, 6, d), lambda i: (i // tiles_per_seq, 0, 0)),
        ],
        out_specs=pl.BlockSpec((tm, d), lambda i: (i, 0)),
        out_shape=jax.ShapeDtypeStruct((t, d), F32),
        scratch_shapes=scratch,
        compiler_params=_cparams("parallel"),
        name="out_proj_conv_residual" if fused_conv else "out_proj_residual",
    )(a, *b_args, w, x2, mod_l)


def _ffn_up_kernel(x_ref, g_ref, mod_ref, w1_ref, w3_ref, s1_ref, s3_ref, o_ref, h_scr, hs_scr):
    j = pl.program_id(1)
    tm = x_ref.shape[0]

    def gated(h8, row_scale):
        a = _dot(h8, w1_ref[...]) * row_scale * s1_ref[...]
        b = _dot(h8, w3_ref[...]) * row_scale * s3_ref[...]
        return ((a * jax.nn.sigmoid(a)) * b).astype(o_ref.dtype)

    @pl.when(j == 0)
    def _():
        rc = tm // NORM_CHUNKS
        for r in range(NORM_CHUNKS):
            rows = slice(r * rc, (r + 1) * rc)
            h = _norm_mod(x_ref[rows, :], g_ref[...], mod_ref[4:5, :], mod_ref[3:4, :])
            amax = jnp.maximum(jnp.max(jnp.abs(h), axis=-1, keepdims=True), F8_TINY)
            h8 = (h * (F8_MAX / amax)).astype(F8)
            row_scale = amax * (1.0 / F8_MAX)
            h_scr[rows, :] = h8
            hs_scr[rows, :] = row_scale
            o_ref[rows, :] = gated(h8, row_scale)

    @pl.when(j > 0)
    def _():
        o_ref[...] = gated(h_scr[...], hs_scr[...])


def _ffn_down_kernel(u_ref, w2_ref, x_ref, mod_ref, *rest, final):
    if final:
        fg_ref, o_ref = rest
    else:
        (o_ref,) = rest
    y = x_ref[...] + mod_ref[5:6, :] * _dot(u_ref[...], w2_ref[...])
    if final:
        y = y * lax.rsqrt(jnp.mean(y * y, axis=-1, keepdims=True) + EPS) * fg_ref[...]
    o_ref[...] = y


def _fp8_weights_kernel(w_ref, q_ref, s_ref):
    w = w_ref[...]
    amax = jnp.maximum(jnp.max(jnp.abs(w), axis=0, keepdims=True), F8_TINY)
    q_ref[...] = (w * (F8_MAX / amax)).astype(q_ref.dtype)
    s_ref[...] = amax * (1.0 / F8_MAX)


def _fp8_weights(w, *, tn=512):
    depth, k, n = w.shape
    return pl.pallas_call(
        _fp8_weights_kernel,
        grid=(depth, n // tn),
        in_specs=[pl.BlockSpec((None, k, tn), lambda l, j: (l, 0, j))],
        out_specs=[pl.BlockSpec((None, k, tn), lambda l, j: (l, 0, j)),
                   pl.BlockSpec((None, 1, tn), lambda l, j: (l, 0, j))],
        out_shape=[jax.ShapeDtypeStruct((depth, k, n), F8), jax.ShapeDtypeStruct((depth, 1, n), F32)],
        compiler_params=_cparams("parallel", "parallel"),
        name="fp8_weight_prep",
    )(w)


def _ffn(x2, g, mod_l, w1, w3, s1, s3, w2, layer, seq, final_g=None, *, tm=1024, tf=512, tm_down=512):
    t, d = x2.shape
    f = w1.shape[2]
    tiles_per_seq = seq // tm
    hidden = pl.pallas_call(
        _ffn_up_kernel,
        grid=(t // tm, f // tf),
        in_specs=[
            pl.BlockSpec((tm, d), lambda i, j: (i, 0)),
            pl.BlockSpec((1, d), lambda i, j: (0, 0)),
            pl.BlockSpec((None, 6, d), lambda i, j: (i // tiles_per_seq, 0, 0)),
            pl.BlockSpec((None, d, tf), lambda i, j: (layer, 0, j)),
            pl.BlockSpec((None, d, tf), lambda i, j: (layer, 0, j)),
            pl.BlockSpec((None, 1, tf), lambda i, j: (layer, 0, j)),
            pl.BlockSpec((None, 1, tf), lambda i, j: (layer, 0, j)),
        ],
        out_specs=pl.BlockSpec((tm, tf), lambda i, j: (i, j)),
        out_shape=jax.ShapeDtypeStruct((t, f), BF16),
        scratch_shapes=[pltpu.VMEM((tm, d), F8), pltpu.VMEM((tm, 1), F32)],
        compiler_params=_cparams("parallel", "arbitrary"),
        name="swiglu_up",
    )(x2, g.reshape(1, d), mod_l, w1, w3, s1, s3)

    final = final_g is not None
    down_tiles_per_seq = seq // tm_down
    in_specs = [
        pl.BlockSpec((tm_down, f), lambda i: (i, 0)),
        pl.BlockSpec((None, f, d), lambda i: (layer, 0, 0), pipeline_mode=pl.Buffered(1)),
        pl.BlockSpec((tm_down, d), lambda i: (i, 0)),
        pl.BlockSpec((None, 6, d), lambda i: (i // down_tiles_per_seq, 0, 0)),
    ]
    args = [hidden, w2, x2, mod_l]
    if final:
        in_specs.append(pl.BlockSpec((1, d), lambda i: (0, 0)))
        args.append(final_g.reshape(1, d))
    return pl.pallas_call(
        functools.partial(_ffn_down_kernel, final=final),
        grid=(t // tm_down,),
        in_specs=in_specs,
        out_specs=pl.BlockSpec((tm_down, d), lambda i: (i, 0)),
        out_shape=jax.ShapeDtypeStruct((t, d), F32),
        compiler_params=_cparams("parallel"),
        name="swiglu_down_final" if final else "swiglu_down",
    )(*args)


def _split3(x):
    hi = x.astype(BF16).astype(F32)
    mid = (x - hi).astype(BF16).astype(F32)
    lo = (x - hi - mid).astype(BF16).astype(F32)
    return hi, mid, lo


def _moba_prep_kernel(slopes_ref, q_ref, k_ref, v_ref, qx_ref, kx_ref, vt_ref, *, nb, blk, grp):
    s, dh = q_ref.shape
    u = slopes_ref[pl.program_id(1)] * (dh ** 0.5)
    kmean = jnp.mean(k_ref[...].astype(F32).reshape(nb, blk, dh), axis=1)
    hi = kmean.astype(BF16)
    lo = (kmean - hi.astype(F32)).astype(BF16)
    q = q_ref[...]
    gate = _dot_nt(hi, q) + _dot_nt(lo, q)
    lanes_per = RANK_QBLOCKS * blk
    pieces = []
    for c in range(s // lanes_per):
        nr = min(nb, RANK_QBLOCKS * (c + 1))
        n_idx = lax.broadcasted_iota(jnp.int32, (nr, lanes_per), 0)
        q_blk = lax.broadcasted_iota(jnp.int32, (nr, lanes_per), 1) // blk + c * RANK_QBLOCKS
        eligible = n_idx < q_blk
        g = jnp.where(eligible, gate[0:nr, c * lanes_per:(c + 1) * lanes_per], NEG_INF)
        rank = jnp.zeros((nr, lanes_per), jnp.int32)
        for m in range(nr):
            gm = g[m:m + 1, :]
            beats = (gm > g) | ((gm == g) & (n_idx > m))
            rank += beats.astype(jnp.int32)
        keep = (eligible & (rank < MOBA_TOPK)) | (n_idx == q_blk)
        piece = jnp.where(keep, 0.0, NEG_INF)
        if nr < nb:
            piece = jnp.concatenate([piece, jnp.full((nb - nr, lanes_per), NEG_INF, F32)], axis=0)
        pieces.append(piece)
    sel_bias = jnp.concatenate(pieces, axis=1) if len(pieces) > 1 else pieces[0]

    off = (lax.broadcasted_iota(jnp.int32, (8, s), 1) % blk).astype(F32) * u
    hi, mid, lo = _split3(off)
    feat = lax.broadcasted_iota(jnp.int32, (8, s), 0)
    pos_rows = jnp.where(feat == 0, hi, jnp.where(feat == 1, mid, jnp.where(feat == 2, lo,
                         jnp.where(feat < 6, 1.0, 0.0))))
    ext_t = jnp.concatenate([sel_bias, pos_rows, jnp.zeros((dh - nb - 8, s), F32)], axis=0)
    chunk = grp * blk
    for ci in range(s // chunk):
        rows = slice(ci * chunk, (ci + 1) * chunk)
        qx_ref[rows, 0:dh] = q_ref[rows, :]
        qx_ref[rows, dh:2 * dh] = ext_t[:, rows].T.astype(qx_ref.dtype)

    col = lax.broadcasted_iota(jnp.int32, (blk, dh), 1)
    khi, kmid, klo = _split3(lax.broadcasted_iota(jnp.int32, (blk, dh), 0).astype(F32) * u)
    tile = jnp.where((col >= nb) & (col < nb + 3), -1.0,
                     jnp.where(col == nb + 3, khi, jnp.where(col == nb + 4, kmid,
                               jnp.where(col == nb + 5, klo, 0.0))))
    kx_ref[:, 0:dh] = k_ref[...]
    for n in range(nb):
        kx_ref[n * blk:(n + 1) * blk, dh:2 * dh] = jnp.where(col == n, 1.0, tile).astype(kx_ref.dtype)

    for n in range(nb):
        v_t = v_ref[n * blk:(n + 1) * blk, :].astype(F32).T.astype(vt_ref.dtype)
        vt_ref[n // grp, :, (n % grp) * blk:(n % grp + 1) * blk] = v_t


def _moba_attn_kernel(slopes_ref, qx_ref, kx_ref, vt_ref, o_ref, m_scr, l_scr, acc_scr, raw_a, raw_b, *,
                      blk, grp, scale):
    h = pl.program_id(1)
    qt = pl.program_id(2)
    slope2 = slopes_ref[h] * LOG2E
    c2 = scale * LOG2E
    causal = (lax.broadcasted_iota(jnp.int32, (blk, blk), 0)
              <= lax.broadcasted_iota(jnp.int32, (blk, blk), 1))

    def scores(b, kg):
        return _dot_nt(kg, qx_ref[b * blk:(b + 1) * blk, :])

    def update(b, tile, vt_g, n0, n_sub, first):
        q_blk = qt * grp + b
        shifts = []
        m_new = None if first else m_scr[b]
        for gi in range(n_sub):
            shift = -slope2 * (blk * (q_blk - (n0 + gi))).astype(F32)
            cm = jnp.max(tile(gi), axis=0, keepdims=True) * c2 + shift
            m_new = cm if m_new is None else jnp.maximum(m_new, cm)
            shifts.append(shift)
        l = None
        probs = []
        for gi, shift in enumerate(shifts):
            p = jnp.exp2(tile(gi) * c2 - (m_new - shift))
            ps = jnp.sum(p, axis=0, keepdims=True)
            l = ps if l is None else l + ps
            probs.append(p.astype(BF16))
        pv = _dot(vt_g, jnp.concatenate(probs, axis=0) if n_sub > 1 else probs[0])
        if first:
            acc_scr[b] = pv
        else:
            alpha = jnp.exp2(m_scr[b] - m_new)
            l = alpha * l_scr[b] + l
            acc_scr[b] = alpha * acc_scr[b] + pv
        m_scr[b] = m_new
        l_scr[b] = l


    base = pl.multiple_of(qt * (grp * blk), grp * blk)
    raws = [scores(b, kx_ref[pl.ds(base, (b + 1) * blk), :]) for b in range(grp)]
    for b in range(grp):
        def diag_tile(gi, b=b):
            r = raws[b][gi * blk:(gi + 1) * blk, :]
            return jnp.where(causal, r, NEG_INF) if gi == b else r
        update(b, diag_tile, vt_ref[qt, :, 0:(b + 1) * blk], qt * grp, b + 1, True)

    def issue(g, buf):
        kg = kx_ref[pl.ds(pl.multiple_of(g * (grp * blk), grp * blk), grp * blk), :]
        for b in range(grp):
            buf[b] = scores(b, kg)

    def consume(g, buf):
        vt_g = vt_ref[g]
        for b in range(grp):
            update(b, lambda gi, b=b: buf[b, gi * blk:(gi + 1) * blk, :], vt_g, g * grp, grp, False)

    n_full = (qt - 1) // 2
    rest = qt - 2 * n_full

    @pl.when(qt > 0)
    def _():
        issue(0, raw_a)

    def body(i, carry):
        issue(2 * i + 1, raw_b)
        consume(2 * i, raw_a)
        issue(2 * i + 2, raw_a)
        consume(2 * i + 1, raw_b)
        return carry

    lax.fori_loop(0, jnp.maximum(n_full, 0), body, 0)

    @pl.when((qt > 0) & (rest == 1))
    def _():
        consume(2 * n_full, raw_a)

    @pl.when((qt > 0) & (rest == 2))
    def _():
        issue(2 * n_full + 1, raw_b)
        consume(2 * n_full, raw_a)
        consume(2 * n_full + 1, raw_b)

    for b in range(grp):
        o_ref[b * blk:(b + 1) * blk, :] = (acc_scr[b] / l_scr[b]).T.astype(o_ref.dtype)


def _moba(proj3, slopes, n_heads):
    bsz, s, _ = proj3.shape
    dh, blk, grp = A_HEAD_DIM, MOBA_BLOCK, MOBA_GROUP
    nb = s // blk
    ng = nb // grp
    tq = grp * blk
    smem = pl.BlockSpec(memory_space=pltpu.SMEM)
    qx, kx, vt = pl.pallas_call(
        functools.partial(_moba_prep_kernel, nb=nb, blk=blk, grp=grp),
        grid=(bsz, n_heads),
        in_specs=[
            smem,
            pl.BlockSpec((None, s, dh), lambda b, h: (b, 0, h)),
            pl.BlockSpec((None, s, dh), lambda b, h: (b, 0, n_heads + h)),
            pl.BlockSpec((None, s, dh), lambda b, h: (b, 0, 2 * n_heads + h)),
        ],
        out_specs=[
            pl.BlockSpec((None, None, s, 2 * dh), lambda b, h: (b, h, 0, 0)),
            pl.BlockSpec((None, None, s, 2 * dh), lambda b, h: (b, h, 0, 0)),
            pl.BlockSpec((None, None, ng, dh, tq), lambda b, h: (b, h, 0, 0, 0)),
        ],
        out_shape=[
            jax.ShapeDtypeStruct((bsz, n_heads, s, 2 * dh), BF16),
            jax.ShapeDtypeStruct((bsz, n_heads, s, 2 * dh), BF16),
            jax.ShapeDtypeStruct((bsz, n_heads, ng, dh, tq), BF16),
        ],
        compiler_params=_cparams("parallel", "parallel"),
        name="moba_select",
    )(slopes, proj3, proj3, proj3)
    return pl.pallas_call(
        functools.partial(_moba_attn_kernel, blk=blk, grp=grp, scale=dh ** -0.5),
        grid=(bsz, n_heads, ng),
        in_specs=[
            smem,
            pl.BlockSpec((None, None, tq, 2 * dh), lambda b, h, i: (b, h, i, 0)),
            pl.BlockSpec((None, None, s, 2 * dh), lambda b, h, i: (b, h, 0, 0)),
            pl.BlockSpec((None, None, ng, dh, tq), lambda b, h, i: (b, h, 0, 0, 0)),
        ],
        out_specs=pl.BlockSpec((None, tq, dh), lambda b, h, i: (b, i, h)),
        out_shape=jax.ShapeDtypeStruct((bsz, s, n_heads * dh), BF16),
        scratch_shapes=[pltpu.VMEM((grp, 1, blk), F32), pltpu.VMEM((grp, 1, blk), F32),
                        pltpu.VMEM((grp, dh, blk), F32),
                        pltpu.VMEM((grp, tq, blk), F32), pltpu.VMEM((grp, tq, blk), F32)],
        compiler_params=_cparams("parallel", "parallel", "arbitrary"),
        name="moba_attention",
    )(slopes, qx, kx, vt)


def _shift_rows(x, d, fill):
    n, c = x.shape
    if d % HALO_ROWS == 0:
        return jnp.concatenate([jnp.full((d, c), fill, x.dtype), x[:n - d]], axis=0)
    rolled = pltpu.roll(x, d, axis=0)
    rows = lax.broadcasted_iota(jnp.int32, x.shape, 0)
    return jnp.where(rows < d, fill, rolled)


def _rglru_kernel(x_ref, xh_ref, gr_ref, cw_ref, cb_ref, w_ref, ba_ref, bx_ref, lam_ref, o_ref,
                  h_scr, *, ts, n_blocks):
    si = pl.program_id(1)

    @pl.when(si == 0)
    def _():
        h_scr[...] = jnp.zeros_like(h_scr)

    xb = x_ref[...]
    cw = cw_ref[...]
    halo = jnp.where(si == 0, 0.0, xh_ref[...].astype(F32))
    halo_row = lax.broadcasted_iota(jnp.int32, halo.shape, 0)
    delta = (lax.broadcasted_iota(jnp.int32, (ts, ts), 0) - lax.broadcasted_iota(jnp.int32, (ts, ts), 1))
    xc = cw[C_CONV - 1:C_CONV, :] * xb.astype(F32) + cb_ref[...]
    head = jnp.zeros_like(halo)
    for j in range(C_CONV - 1):
        d = C_CONV - 1 - j
        shifted = _dot(jnp.where(delta == d, 1.0, 0.0).astype(BF16), xb)
        xc += cw[j:j + 1, :] * shifted
        head += cw[j:j + 1, :] * jnp.where(halo_row < d, pltpu.roll(halo, d, axis=0), 0.0)
    xc = jnp.concatenate([xc[:HALO_ROWS] + head, xc[HALO_ROWS:]], axis=0)

    bd = xc.shape[1] // n_blocks
    xcb = xc.astype(BF16)
    gates = [_dot(xcb[:, g * bd:(g + 1) * bd], w_ref[g]) for g in range(n_blocks)]
    r = jax.nn.sigmoid(jnp.concatenate([gt[:, :bd] for gt in gates], axis=1) + ba_ref[...])
    i = jax.nn.sigmoid(jnp.concatenate([gt[:, bd:] for gt in gates], axis=1) + bx_ref[...])
    log_a = (-RG_C) * r * jax.nn.softplus(-lam_ref[...])
    a = jnp.exp(log_a)
    z = -jnp.tanh(log_a) * (a * a + 1.0)
    u = jnp.where(z > 0.0, z * lax.rsqrt(z), 0.0) * (i * xc)

    d = 1
    while d < ts:
        u = a * _shift_rows(u, d, 0.0) + u
        a = a * _shift_rows(a, d, 1.0)
        d *= 2
    hs = a * h_scr[...] + u
    h_scr[...] = hs[ts - 1:ts, :]
    o_ref[...] = (hs * jax.nn.gelu(gr_ref[...].astype(F32))).astype(o_ref.dtype)


def _rglru(proj3, conv_w, conv_b, w_cat, ba, bx, lam, *, ts=256):
    bsz, s, _ = proj3.shape
    c = conv_w.shape[1]
    n_blocks, bd, _ = w_cat.shape
    hb = ts // HALO_ROWS
    vec = pl.BlockSpec((1, c), lambda b, i: (0, 0))
    return pl.pallas_call(
        functools.partial(_rglru_kernel, ts=ts, n_blocks=n_blocks),
        grid=(bsz, s // ts),
        in_specs=[
            pl.BlockSpec((None, ts, c), lambda b, i: (b, i, 0)),
            pl.BlockSpec((None, HALO_ROWS, c), lambda b, i: (b, jnp.maximum(i * hb - 1, 0), 0)),
            pl.BlockSpec((None, ts, c), lambda b, i: (b, i, 1)),
            pl.BlockSpec((C_CONV, c), lambda b, i: (0, 0)),
            vec,
            pl.BlockSpec((n_blocks, bd, 2 * bd), lambda b, i: (0, 0, 0)),
            vec, vec, vec,
        ],
        out_specs=pl.BlockSpec((None, ts, c), lambda b, i: (b, i, 0)),
        out_shape=jax.ShapeDtypeStruct((bsz, s, c), BF16),
        scratch_shapes=[pltpu.VMEM((1, c), F32)],
        compiler_params=_cparams("parallel", "arbitrary"),
        name="rglru_scan",
    )(proj3, proj3, proj3, conv_w, conv_b.reshape(1, c), w_cat, ba.reshape(1, c), bx.reshape(1, c),
      lam.reshape(1, c))


def _mlstm_kernel(q_ref, k_ref, v_ref, og_ref, gt_ref, o_ref, c_scr, n_scr, m_scr, *, chunk, nh, dh):
    ci = pl.program_id(1)

    @pl.when(ci == 0)
    def _():
        c_scr[...] = jnp.zeros_like(c_scr)
        n_scr[...] = jnp.zeros_like(n_scr)
        m_scr[...] = jnp.zeros_like(m_scr)

    gates = gt_ref[...]
    bcum = jax.nn.log_sigmoid(gates)
    d = 1
    while d < chunk:
        bcum = bcum + _shift_rows(bcum, d, 0.0)
        d *= 2
    lane = lax.broadcasted_iota(jnp.int32, gates.shape, 1)
    rows_t = jnp.where(lane < nh, gates, bcum).T
    r_idx = lax.broadcasted_iota(jnp.int32, (chunk, chunk), 0)
    c_idx = lax.broadcasted_iota(jnp.int32, (chunk, chunk), 1)
    causal = c_idx <= r_idx
    k_scale = dh ** -0.5

    head_cols = [slice(h * dh, (h + 1) * dh) for h in range(nh)]
    qs = [q_ref[:, cols] for cols in head_cols]
    kss = [k_ref[:, cols] * jnp.asarray(k_scale, BF16) for cols in head_cols]
    vs = [v_ref[:, cols] for cols in head_cols]
    c_prevs = [c_scr[h] for h in range(nh)]
    qk = [_dot_nt(qs[h], kss[h]) for h in range(nh)]
    qc = [_dot(qs[h], c_prevs[h].astype(BF16)) for h in range(nh)]

    stage = []
    for h in range(nh):
        i_col = gates[:, h:h + 1]
        b_col = bcum[:, nh + h:nh + h + 1]
        i_row = rows_t[h:h + 1, :]
        b_row = rows_t[nh + h:nh + h + 1, :]
        m_prev = m_scr[h]

        dmat = jnp.where(causal, b_col - b_row + i_row, NEG_INF)
        inter = b_col + m_prev
        m_t = jnp.maximum(inter, jnp.max(dmat, axis=-1, keepdims=True))
        w_inter = jnp.exp(inter - m_t)
        s_qk = qk[h] * jnp.exp(dmat - m_t)
        n_prev = n_scr[h]
        q_n = jnp.sum(qs[h].astype(F32) * n_prev, axis=-1, keepdims=True)
        den = w_inter * q_n + jnp.sum(s_qk, axis=-1, keepdims=True)

        b_last = b_col[chunk - 1:chunk, :]
        w_s = b_last - b_col + i_col
        m_new = jnp.maximum(b_last + m_prev, jnp.max(w_s, axis=0, keepdims=True))
        decay = jnp.exp(b_last + m_prev - m_new)
        kw = jnp.exp(w_s - m_new) * kss[h].astype(F32)
        n_scr[h] = decay * n_prev + jnp.sum(kw, axis=0, keepdims=True)
        m_scr[h] = m_new
        stage.append((s_qk.astype(BF16), kw.astype(BF16), w_inter, den, m_t, decay))

    sv = [_dot(stage[h][0], vs[h]) for h in range(nh)]
    kv = [_dot_tn(stage[h][1], vs[h]) for h in range(nh)]
    for h in range(nh):
        _, _, w_inter, den, m_t, decay = stage[h]
        hid = (w_inter * qc[h] + sv[h]) / jnp.maximum(jnp.abs(den), jnp.exp(-m_t))
        c_scr[h] = decay * c_prevs[h] + kv[h]
        cols = head_cols[h]
        o_ref[:, cols] = (hid * jax.nn.sigmoid(og_ref[:, cols].astype(F32))).astype(o_ref.dtype)


def _mlstm(proj3, gates3, *, col0, nh, chunk=MLSTM_CHUNK):
    bsz, s, _ = proj3.shape
    dh = 256
    w = nh * dh
    cb = col0 // w
    blk = lambda k: pl.BlockSpec((None, chunk, w), lambda b, i: (b, i, cb + k))
    return pl.pallas_call(
        functools.partial(_mlstm_kernel, chunk=chunk, nh=nh, dh=dh),
        grid=(bsz, s // chunk),
        in_specs=[blk(0), blk(1), blk(2), blk(3),
                  pl.BlockSpec((None, chunk, GATE_LANES), lambda b, i: (b, i, 0))],
        out_specs=pl.BlockSpec((None, chunk, w), lambda b, i: (b, i, 0)),
        out_shape=jax.ShapeDtypeStruct((bsz, s, w), BF16),
        scratch_shapes=[pltpu.VMEM((nh, dh, dh), F32), pltpu.VMEM((nh, 1, dh), F32),
                        pltpu.VMEM((nh, 1, 1), F32)],
        compiler_params=_cparams("parallel", "arbitrary"),
        name="mlstm_chunkwise",
    )(proj3, proj3, proj3, proj3, gates3)


def kernel(x, c, mod_w, mod_b, norm_mix_g, norm_ffn_g, ffn_w1, ffn_w3, ffn_w2, ab_w_in, ab_conv_w, ab_w_out, cd_w_in, cd_gate_b, rg_conv_w, rg_conv_b, rg_wa, rg_ba, rg_wx, rg_bx, rg_lambda, cd_w_out, final_norm_g):
    bsz, seq, d = x.shape
    depth = mod_w.shape[0]
    t = bsz * seq
    half = d // 2
    a_heads = half // A_HEAD_DIM
    slopes = 2.0 ** (-8.0 * jnp.arange(1, a_heads + 1, dtype=F32) / a_heads)

    mod = _modulation(c, mod_w, mod_b)
    ffn_w1_q, ffn_s1 = _fp8_weights(ffn_w1)
    ffn_w3_q, ffn_s3 = _fp8_weights(ffn_w3)
    ffn_w2_b = ffn_w2.astype(BF16)
    ab_w_in_b, ab_w_out_b = ab_w_in.astype(BF16), ab_w_out.astype(BF16)
    cd_w_in_b, cd_w_out_b = cd_w_in.astype(BF16), cd_w_out.astype(BF16)
    x2 = x.reshape(t, d)
    for layer in range(depth):
        mod_l = mod[layer]
        if layer % 2 == 0:
            e = layer // 2
            n_main = 6 * half
            proj = _in_proj(x2, norm_mix_g[layer], mod_l, ab_w_in_b, e, n_main, seq, shift_row=0, scale_row=1)
            proj3 = proj.reshape(bsz, seq, n_main)
            y_a = _moba(proj3, slopes, a_heads).reshape(t, half)
            y_b, conv = proj, dict(conv_w=ab_conv_w[e], conv_col0=3 * half)
            w_out, w_idx = ab_w_out_b, e
        else:
            o = layer // 2
            n_main = 2 * half + 4 * half
            n_g = cd_w_in.shape[2] - n_main
            w_gate = jnp.zeros((d, GATE_LANES), BF16).at[:, :n_g].set(cd_w_in[o, :, n_main:].astype(BF16))
            b_gate = jnp.zeros((1, GATE_LANES), F32).at[0, :n_g].set(cd_gate_b[o])
            proj, gates = _in_proj(x2, norm_mix_g[layer], mod_l, cd_w_in_b, o, n_main, seq,
                                   shift_row=0, scale_row=1, w_gate=w_gate, b_gate=b_gate)
            proj3 = proj.reshape(bsz, seq, n_main)
            w_cat = jnp.concatenate([rg_wa[o], rg_wx[o]], axis=-1).astype(BF16)
            y_a = _rglru(proj3, rg_conv_w[o], rg_conv_b[o], w_cat, rg_ba[o], rg_bx[o],
                         rg_lambda[o]).reshape(t, half)
            y_b = _mlstm(proj3, gates.reshape(bsz, seq, GATE_LANES), col0=2 * half,
                         nh=D_HEADS).reshape(t, half)
            w_out, w_idx, conv = cd_w_out_b, o, {}
        x2 = _out_proj(y_a, y_b, w_out, w_idx, x2, mod_l, seq, gate_row=2, **conv)
        final_g = final_norm_g if layer == depth - 1 else None
        x2 = _ffn(x2, norm_ffn_g[layer], mod_l, ffn_w1_q, ffn_w3_q, ffn_s1, ffn_s3, ffn_w2_b,
                  layer, seq, final_g)
    return x2.reshape(bsz, seq, d)
```

```python
import functools

import jax
import jax.numpy as jnp
from jax import lax
from jax.experimental import pallas as pl
from jax.experimental.pallas import tpu as pltpu

F32 = jnp.float32
BF16 = jnp.bfloat16
F8 = jnp.float8_e4m3fn
F8_MAX = 448.0
F8_TINY = 1e-30

EPS = 1e-6
NEG_INF = -1e30
LOG2E = 1.4426950408889634

A_HEAD_DIM = 128
MOBA_BLOCK = 256
MOBA_TOPK = 3
MOBA_GROUP = 4
RANK_QBLOCKS = 8
VT_PAD_ROWS = 16
B_CONV = 3
C_BLOCKS = 8
C_CONV = 4
RG_C = 8.0
D_HEADS = 4
MLSTM_CHUNK = 256
GATE_LANES = 128

VMEM_LIMIT_BYTES = 60 * 1024 * 1024
HALO_ROWS = 8
NORM_CHUNKS = 4


def _cparams(*sem):
    return pltpu.CompilerParams(dimension_semantics=sem, vmem_limit_bytes=VMEM_LIMIT_BYTES)


def _dot(a, b):
    return jnp.dot(a, b, preferred_element_type=F32)


def _dot_nt(a, b):
    return lax.dot_general(a, b, (((1,), (1,)), ((), ())), preferred_element_type=F32)


def _dot_tn(a, b):
    return lax.dot_general(a, b, (((0,), (0,)), ((), ())), preferred_element_type=F32)


def _norm_mod(x, g, scale, shift):
    xn = x * lax.rsqrt(jnp.mean(x * x, axis=-1, keepdims=True) + EPS)
    return (xn * g) * (1.0 + scale) + shift


def _mod_kernel(c_ref, w_ref, b_ref, o_ref):
    c = c_ref[...]
    cond = c * jax.nn.sigmoid(c)
    o_ref[...] = _dot(cond.astype(BF16), w_ref[...].astype(BF16)) + b_ref[...]


def _modulation(c, mod_w, mod_b):
    depth, d, n = mod_w.shape
    bsz = c.shape[0]
    rows = 8
    c8 = jnp.zeros((rows, d), F32).at[:bsz].set(c)
    tn = 1024
    out = pl.pallas_call(
        _mod_kernel,
        grid=(depth, n // tn),
        in_specs=[
            pl.BlockSpec((rows, d), lambda l, j: (0, 0)),
            pl.BlockSpec((None, d, tn), lambda l, j: (l, 0, j)),
            pl.BlockSpec((None, 1, tn), lambda l, j: (l, 0, j)),
        ],
        out_specs=pl.BlockSpec((None, rows, tn), lambda l, j: (l, 0, j)),
        out_shape=jax.ShapeDtypeStruct((depth, rows, n), F32),
        compiler_params=_cparams("arbitrary", "arbitrary"),
        name="adaln_modulation",
    )(c8, mod_w, mod_b.reshape(depth, 1, n))
    return out[:, :bsz].reshape(depth, bsz, 6, d)


def _in_proj_kernel(x_ref, g_ref, mod_ref, w_ref, *rest, shift_row, scale_row, with_gates):
    if with_gates:
        wg_ref, bg_ref, o_ref, og_ref, h_scr = rest
    else:
        o_ref, h_scr = rest
    j = pl.program_id(1)

    @pl.when(j == 0)
    def _():
        rc = x_ref.shape[0] // NORM_CHUNKS
        for r in range(NORM_CHUNKS):
            rows = slice(r * rc, (r + 1) * rc)
            h = _norm_mod(x_ref[rows, :], g_ref[...], mod_ref[scale_row:scale_row + 1, :],
                          mod_ref[shift_row:shift_row + 1, :])
            hb = h.astype(BF16)
            h_scr[rows, :] = hb
            o_ref[rows, :] = _dot(hb, w_ref[...]).astype(o_ref.dtype)
            if with_gates:
                og_ref[rows, :] = _dot(hb, wg_ref[...]) + bg_ref[...]

    @pl.when(j > 0)
    def _():
        o_ref[...] = _dot(h_scr[...], w_ref[...]).astype(o_ref.dtype)


def _in_proj(x2, g, mod_l, w, idx, n, seq, *, shift_row, scale_row, w_gate=None, b_gate=None, tm=1024, tn=2048):
    t, d = x2.shape
    tiles_per_seq = seq // tm
    with_gates = w_gate is not None
    in_specs = [
        pl.BlockSpec((tm, d), lambda i, j: (i, 0)),
        pl.BlockSpec((1, d), lambda i, j: (0, 0)),
        pl.BlockSpec((None, 6, d), lambda i, j: (i // tiles_per_seq, 0, 0)),
        pl.BlockSpec((None, d, tn), lambda i, j: (idx, 0, j)),
    ]
    args = [x2, g.reshape(1, d), mod_l, w]
    out_specs = pl.BlockSpec((tm, tn), lambda i, j: (i, j))
    out_shape = jax.ShapeDtypeStruct((t, n), BF16)
    if with_gates:
        in_specs += [pl.BlockSpec((d, GATE_LANES), lambda i, j: (0, 0)),
                     pl.BlockSpec((1, GATE_LANES), lambda i, j: (0, 0))]
        args += [w_gate, b_gate]
        out_specs = [out_specs, pl.BlockSpec((tm, GATE_LANES), lambda i, j: (i, 0))]
        out_shape = [out_shape, jax.ShapeDtypeStruct((t, GATE_LANES), F32)]
    return pl.pallas_call(
        functools.partial(_in_proj_kernel, shift_row=shift_row, scale_row=scale_row, with_gates=with_gates),
        grid=(t // tm, n // tn),
        in_specs=in_specs,
        out_specs=out_specs,
        out_shape=out_shape,
        scratch_shapes=[pltpu.VMEM((tm, d), BF16)],
        compiler_params=_cparams("parallel", "arbitrary"),
        name="norm_in_proj_gates" if with_gates else "norm_in_proj",
    )(*args)


def _gated_conv_tile(gb_ref, gc_ref, xb_ref, gch_ref, xbh_ref, w_ref, u_scr, first):
    tm = gc_ref.shape[0]
    u_scr[0:HALO_ROWS, :] = jnp.where(first, 0.0, gch_ref[...].astype(F32) * xbh_ref[...].astype(F32))
    u = gc_ref[...].astype(F32) * xb_ref[...].astype(F32)
    u_scr[HALO_ROWS:, :] = u
    w = w_ref[...]
    conv = w[B_CONV - 1:B_CONV, :] * u
    for j in range(B_CONV - 1):
        off = HALO_ROWS - (B_CONV - 1) + j
        conv += w[j:j + 1, :] * u_scr[off:off + tm, :]
    return gb_ref[...].astype(F32) * conv


def _out_proj_kernel(a_ref, *rest, gate_row, half, conv_tiles_per_seq):
    if conv_tiles_per_seq:
        gb_ref, gc_ref, xb_ref, gch_ref, xbh_ref, cw_ref, w_ref, x_ref, mod_ref, o_ref, u_scr = rest
    else:
        b_ref, w_ref, x_ref, mod_ref, o_ref = rest
    acc = _dot(a_ref[...], w_ref[0:half, :])
    if conv_tiles_per_seq:
        first = (pl.program_id(0) % conv_tiles_per_seq) == 0
        b = _gated_conv_tile(gb_ref, gc_ref, xb_ref, gch_ref, xbh_ref, cw_ref, u_scr, first).astype(BF16)
    else:
        b = b_ref[...]
    acc += _dot(b, w_ref[half:, :])
    o_ref[...] = x_ref[...] + mod_ref[gate_row:gate_row + 1, :] * acc


def _out_proj(a, b, w, idx, x2, mod_l, seq, *, gate_row, conv_w=None, conv_col0=None, tm=512):
    t, d = x2.shape
    half = a.shape[1]
    tiles_per_seq = seq // tm
    fused_conv = conv_w is not None
    row_block = pl.BlockSpec((tm, half), lambda i: (i, 0))
    if fused_conv:
        cb = conv_col0 // half
        hb = tm // HALO_ROWS
        halo = lambda k: pl.BlockSpec((HALO_ROWS, half), lambda i: (jnp.maximum(i * hb - 1, 0), cb + k))
        b_specs = [pl.BlockSpec((tm, half), lambda i: (i, cb)), pl.BlockSpec((tm, half), lambda i: (i, cb + 1)),
                   pl.BlockSpec((tm, half), lambda i: (i, cb + 2)), halo(1), halo(2),
                   pl.BlockSpec((B_CONV, half), lambda i: (0, 0))]
        b_args = [b, b, b, b, b, conv_w]
        scratch = [pltpu.VMEM((tm + HALO_ROWS, half), F32)]
    else:
        b_specs, b_args, scratch = [row_block], [b], []
    return pl.pallas_call(
        functools.partial(_out_proj_kernel, gate_row=gate_row, half=half,
                          conv_tiles_per_seq=tiles_per_seq if fused_conv else 0),
        grid=(t // tm,),
        in_specs=[row_block] + b_specs + [
            pl.BlockSpec((None, 2 * half, d), lambda i: (idx, 0, 0)),
            pl.BlockSpec((tm, d), lambda i: (i, 0)),
            pl.BlockSpec((None, 6, d), lambda i: (i // tiles_per_seq, 0, 0)),
        ],
        out_specs=pl.BlockSpec((tm, d), lambda i: (i, 0)),
        out_shape=jax.ShapeDtypeStruct((t, d), F32),
        scratch_shapes=scratch,
        compiler_params=_cparams("parallel"),
        name="out_proj_conv_residual" if fused_conv else "out_proj_residual",
    )(a, *b_args, w, x2, mod_l)


def _ffn_up_kernel(x_ref, g_ref, mod_ref, w1_ref, w3_ref, s1_ref, s3_ref, o_ref, h_scr, hs_scr):
    j = pl.program_id(1)
    tm = x_ref.shape[0]

    def gated(h8, row_scale):
        a = _dot(h8, w1_ref[...]) * row_scale * s1_ref[...]
        b = _dot(h8, w3_ref[...]) * row_scale * s3_ref[...]
        return ((a * jax.nn.sigmoid(a)) * b).astype(o_ref.dtype)

    @pl.when(j == 0)
    def _():
        rc = tm // NORM_CHUNKS
        for r in range(NORM_CHUNKS):
            rows = slice(r * rc, (r + 1) * rc)
            h = _norm_mod(x_ref[rows, :], g_ref[...], mod_ref[4:5, :], mod_ref[3:4, :])
            amax = jnp.maximum(jnp.max(jnp.abs(h), axis=-1, keepdims=True), F8_TINY)
            h8 = (h * (F8_MAX / amax)).astype(F8)
            row_scale = amax * (1.0 / F8_MAX)
            h_scr[rows, :] = h8
            hs_scr[rows, :] = row_scale
            o_ref[rows, :] = gated(h8, row_scale)

    @pl.when(j > 0)
    def _():
        o_ref[...] = gated(h_scr[...], hs_scr[...])


def _ffn_down_kernel(u_ref, w2_ref, x_ref, mod_ref, *rest, final):
    if final:
        fg_ref, o_ref = rest
    else:
        (o_ref,) = rest
    y = x_ref[...] + mod_ref[5:6, :] * _dot(u_ref[...], w2_ref[...])
    if final:
        y = y * lax.rsqrt(jnp.mean(y * y, axis=-1, keepdims=True) + EPS) * fg_ref[...]
    o_ref[...] = y


def _fp8_weights_kernel(w_ref, q_ref, s_ref):
    w = w_ref[...]
    amax = jnp.maximum(jnp.max(jnp.abs(w), axis=0, keepdims=True), F8_TINY)
    q_ref[...] = (w * (F8_MAX / amax)).astype(q_ref.dtype)
    s_ref[...] = amax * (1.0 / F8_MAX)


def _fp8_weights(w, *, tn=512):
    depth, k, n = w.shape
    return pl.pallas_call(
        _fp8_weights_kernel,
        grid=(depth, n // tn),
        in_specs=[pl.BlockSpec((None, k, tn), lambda l, j: (l, 0, j))],
        out_specs=[pl.BlockSpec((None, k, tn), lambda l, j: (l, 0, j)),
                   pl.BlockSpec((None, 1, tn), lambda l, j: (l, 0, j))],
        out_shape=[jax.ShapeDtypeStruct((depth, k, n), F8), jax.ShapeDtypeStruct((depth, 1, n), F32)],
        compiler_params=_cparams("parallel", "parallel"),
        name="fp8_weight_prep",
    )(w)


def _ffn(x2, g, mod_l, w1, w3, s1, s3, w2, layer, seq, final_g=None, *, tm=1024, tf=512, tm_down=512):
    t, d = x2.shape
    f = w1.shape[2]
    tiles_per_seq = seq // tm
    hidden = pl.pallas_call(
        _ffn_up_kernel,
        grid=(t // tm, f // tf),
        in_specs=[
            pl.BlockSpec((tm, d), lambda i, j: (i, 0)),
            pl.BlockSpec((1, d), lambda i, j: (0, 0)),
            pl.BlockSpec((None, 6, d), lambda i, j: (i // tiles_per_seq, 0, 0)),
            pl.BlockSpec((None, d, tf), lambda i, j: (layer, 0, j)),
            pl.BlockSpec((None, d, tf), lambda i, j: (layer, 0, j)),
            pl.BlockSpec((None, 1, tf), lambda i, j: (layer, 0, j)),
            pl.BlockSpec((None, 1, tf), lambda i, j: (layer, 0, j)),
        ],
        out_specs=pl.BlockSpec((tm, tf), lambda i, j: (i, j)),
        out_shape=jax.ShapeDtypeStruct((t, f), BF16),
        scratch_shapes=[pltpu.VMEM((tm, d), F8), pltpu.VMEM((tm, 1), F32)],
        compiler_params=_cparams("parallel", "arbitrary"),
        name="swiglu_up",
    )(x2, g.reshape(1, d), mod_l, w1, w3, s1, s3)

    final = final_g is not None
    down_tiles_per_seq = seq // tm_down
    in_specs = [
        pl.BlockSpec((tm_down, f), lambda i: (i, 0)),
        pl.BlockSpec((None, f, d), lambda i: (layer, 0, 0), pipeline_mode=pl.Buffered(1)),
        pl.BlockSpec((tm_down, d), lambda i: (i, 0)),
        pl.BlockSpec((None, 6, d), lambda i: (i // down_tiles_per_seq, 0, 0)),
    ]
    args = [hidden, w2, x2, mod_l]
    if final:
        in_specs.append(pl.BlockSpec((1, d), lambda i: (0, 0)))
        args.append(final_g.reshape(1, d))
    return pl.pallas_call(
        functools.partial(_ffn_down_kernel, final=final),
        grid=(t // tm_down,),
        in_specs=in_specs,
        out_specs=pl.BlockSpec((tm_down, d), lambda i: (i, 0)),
        out_shape=jax.ShapeDtypeStruct((t, d), F32),
        compiler_params=_cparams("parallel"),
        name="swiglu_down_final" if final else "swiglu_down",
    )(*args)


def _split3(x):
    hi = x.astype(BF16).astype(F32)
    mid = (x - hi).astype(BF16).astype(F32)
    lo = (x - hi - mid).astype(BF16).astype(F32)
    return hi, mid, lo


def _moba_prep_kernel(slopes_ref, q_ref, k_ref, v_ref, qx_ref, kx_ref, vt_ref, *, nb, blk, grp):
    s, dh = q_ref.shape
    u = slopes_ref[pl.program_id(1)] * (dh ** 0.5)
    kmean = jnp.mean(k_ref[...].astype(F32).reshape(nb, blk, dh), axis=1)
    hi = kmean.astype(BF16)
    lo = (kmean - hi.astype(F32)).astype(BF16)
    q = q_ref[...]
    gate = _dot_nt(hi, q) + _dot_nt(lo, q)
    lanes_per = RANK_QBLOCKS * blk
    pieces = []
    for c in range(s // lanes_per):
        nr = min(nb, RANK_QBLOCKS * (c + 1))
        n_idx = lax.broadcasted_iota(jnp.int32, (nr, lanes_per), 0)
        q_blk = lax.broadcasted_iota(jnp.int32, (nr, lanes_per), 1) // blk + c * RANK_QBLOCKS
        eligible = n_idx < q_blk
        g = jnp.where(eligible, gate[0:nr, c * lanes_per:(c + 1) * lanes_per], NEG_INF)
        rank = jnp.zeros((nr, lanes_per), jnp.int32)
        for m in range(nr):
            gm = g[m:m + 1, :]
            beats = (gm > g) | ((gm == g) & (n_idx > m))
            rank += beats.astype(jnp.int32)
        keep = (eligible & (rank < MOBA_TOPK)) | (n_idx == q_blk)
        piece = jnp.where(keep, 0.0, NEG_INF)
        if nr < nb:
            piece = jnp.concatenate([piece, jnp.full((nb - nr, lanes_per), NEG_INF, F32)], axis=0)
        pieces.append(piece)
    sel_bias = jnp.concatenate(pieces, axis=1) if len(pieces) > 1 else pieces[0]

    off = (lax.broadcasted_iota(jnp.int32, (8, s), 1) % blk).astype(F32) * u
    hi, mid, lo = _split3(off)
    feat = lax.broadcasted_iota(jnp.int32, (8, s), 0)
    pos_rows = jnp.where(feat == 0, hi, jnp.where(feat == 1, mid, jnp.where(feat == 2, lo,
                         jnp.where(feat < 6, 1.0, 0.0))))
    ext_t = jnp.concatenate([sel_bias, pos_rows, jnp.zeros((dh - nb - 8, s), F32)], axis=0)
    chunk = grp * blk
    for ci in range(s // chunk):
        rows = slice(ci * chunk, (ci + 1) * chunk)
        qx_ref[rows, 0:dh] = q_ref[rows, :]
        qx_ref[rows, dh:2 * dh] = ext_t[:, rows].T.astype(qx_ref.dtype)

    col = lax.broadcasted_iota(jnp.int32, (blk, dh), 1)
    khi, kmid, klo = _split3(lax.broadcasted_iota(jnp.int32, (blk, dh), 0).astype(F32) * u)
    tile = jnp.where((col >= nb) & (col < nb + 3), -1.0,
                     jnp.where(col == nb + 3, khi, jnp.where(col == nb + 4, kmid,
                               jnp.where(col == nb + 5, klo, 0.0))))
    kx_ref[:, 0:dh] = k_ref[...]
    for n in range(nb):
        kx_ref[n * blk:(n + 1) * blk, dh:2 * dh] = jnp.where(col == n, 1.0, tile).astype(kx_ref.dtype)

    pad_row = lax.broadcasted_iota(jnp.int32, (VT_PAD_ROWS, blk), 0)
    ones_pad = jnp.where(pad_row == 0, 1.0, 0.0).astype(vt_ref.dtype)
    for n in range(nb):
        v_t = v_ref[n * blk:(n + 1) * blk, :].astype(F32).T.astype(vt_ref.dtype)
        cols = slice((n % grp) * blk, (n % grp + 1) * blk)
        vt_ref[n // grp, 0:dh, cols] = v_t
        vt_ref[n // grp, dh:dh + VT_PAD_ROWS, cols] = ones_pad


def _moba_attn_kernel(slopes_ref, qx_ref, kx_ref, vt_ref, o_ref, m_scr, acc_scr, raw_a, raw_b, *,
                      blk, grp, scale):
    h = pl.program_id(1)
    qt = pl.program_id(2)
    slope2 = slopes_ref[h] * LOG2E
    c2 = scale * LOG2E
    causal = (lax.broadcasted_iota(jnp.int32, (blk, blk), 0)
              <= lax.broadcasted_iota(jnp.int32, (blk, blk), 1))

    def scores(b, kg):
        return _dot_nt(kg, qx_ref[b * blk:(b + 1) * blk, :])

    def update(b, tile, vt_g, n0, n_sub, first):
        q_blk = qt * grp + b
        shifts = []
        m_new = None if first else m_scr[b]
        for gi in range(n_sub):
            shift = -slope2 * (blk * (q_blk - (n0 + gi))).astype(F32)
            cm = jnp.max(tile(gi), axis=0, keepdims=True) * c2 + shift
            m_new = cm if m_new is None else jnp.maximum(m_new, cm)
            shifts.append(shift)
        probs = [jnp.exp2(tile(gi) * c2 - (m_new - shift)).astype(BF16) for gi, shift in enumerate(shifts)]
        pv = _dot(vt_g, jnp.concatenate(probs, axis=0) if n_sub > 1 else probs[0])
        if first:
            acc_scr[b] = pv
        else:
            acc_scr[b] = jnp.exp2(m_scr[b] - m_new) * acc_scr[b] + pv
        m_scr[b] = m_new


    base = pl.multiple_of(qt * (grp * blk), grp * blk)
    raws = [scores(b, kx_ref[pl.ds(base, (b + 1) * blk), :]) for b in range(grp)]
    for b in range(grp):
        def diag_tile(gi, b=b):
            r = raws[b][gi * blk:(gi + 1) * blk, :]
            return jnp.where(causal, r, NEG_INF) if gi == b else r
        update(b, diag_tile, vt_ref[qt, :, 0:(b + 1) * blk], qt * grp, b + 1, True)

    def issue(g, buf):
        kg = kx_ref[pl.ds(pl.multiple_of(g * (grp * blk), grp * blk), grp * blk), :]
        for b in range(grp):
            buf[b] = scores(b, kg)

    def consume(g, buf):
        vt_g = vt_ref[g]
        for b in range(grp):
            update(b, lambda gi, b=b: buf[b, gi * blk:(gi + 1) * blk, :], vt_g, g * grp, grp, False)

    n_full = (qt - 1) // 2
    rest = qt - 2 * n_full

    @pl.when(qt > 0)
    def _():
        issue(0, raw_a)

    def body(i, carry):
        issue(2 * i + 1, raw_b)
        consume(2 * i, raw_a)
        issue(2 * i + 2, raw_a)
        consume(2 * i + 1, raw_b)
        return carry

    lax.fori_loop(0, jnp.maximum(n_full, 0), body, 0)

    @pl.when((qt > 0) & (rest == 1))
    def _():
        consume(2 * n_full, raw_a)

    @pl.when((qt > 0) & (rest == 2))
    def _():
        issue(2 * n_full + 1, raw_b)
        consume(2 * n_full, raw_a)
        consume(2 * n_full + 1, raw_b)

    dh = o_ref.shape[1]
    for b in range(grp):
        acc = acc_scr[b]
        o_ref[b * blk:(b + 1) * blk, :] = (acc[0:dh, :] / acc[dh:dh + 1, :]).T.astype(o_ref.dtype)


def _moba(proj3, slopes, n_heads):
    bsz, s, _ = proj3.shape
    dh, blk, grp = A_HEAD_DIM, MOBA_BLOCK, MOBA_GROUP
    nb = s // blk
    ng = nb // grp
    tq = grp * blk
    smem = pl.BlockSpec(memory_space=pltpu.SMEM)
    qx, kx, vt = pl.pallas_call(
        functools.partial(_moba_prep_kernel, nb=nb, blk=blk, grp=grp),
        grid=(bsz, n_heads),
        in_specs=[
            smem,
            pl.BlockSpec((None, s, dh), lambda b, h: (b, 0, h)),
            pl.BlockSpec((None, s, dh), lambda b, h: (b, 0, n_heads + h)),
            pl.BlockSpec((None, s, dh), lambda b, h: (b, 0, 2 * n_heads + h)),
        ],
        out_specs=[
            pl.BlockSpec((None, None, s, 2 * dh), lambda b, h: (b, h, 0, 0)),
            pl.BlockSpec((None, None, s, 2 * dh), lambda b, h: (b, h, 0, 0)),
            pl.BlockSpec((None, None, ng, dh + VT_PAD_ROWS, tq), lambda b, h: (b, h, 0, 0, 0)),
        ],
        out_shape=[
            jax.ShapeDtypeStruct((bsz, n_heads, s, 2 * dh), BF16),
            jax.ShapeDtypeStruct((bsz, n_heads, s, 2 * dh), BF16),
            jax.ShapeDtypeStruct((bsz, n_heads, ng, dh + VT_PAD_ROWS, tq), BF16),
        ],
        compiler_params=_cparams("parallel", "parallel"),
        name="moba_select",
    )(slopes, proj3, proj3, proj3)
    return pl.pallas_call(
        functools.partial(_moba_attn_kernel, blk=blk, grp=grp, scale=dh ** -0.5),
        grid=(bsz, n_heads, ng),
        in_specs=[
            smem,
            pl.BlockSpec((None, None, tq, 2 * dh), lambda b, h, i: (b, h, i, 0)),
            pl.BlockSpec((None, None, s, 2 * dh), lambda b, h, i: (b, h, 0, 0)),
            pl.BlockSpec((None, None, ng, dh + VT_PAD_ROWS, tq), lambda b, h, i: (b, h, 0, 0, 0)),
        ],
        out_specs=pl.BlockSpec((None, tq, dh), lambda b, h, i: (b, i, h)),
        out_shape=jax.ShapeDtypeStruct((bsz, s, n_heads * dh), BF16),
        scratch_shapes=[pltpu.VMEM((grp, 1, blk), F32), pltpu.VMEM((grp, dh + VT_PAD_ROWS, blk), F32),
                        pltpu.VMEM((grp, tq, blk), F32), pltpu.VMEM((grp, tq, blk), F32)],
        compiler_params=_cparams("parallel", "parallel", "arbitrary"),
        name="moba_attention",
    )(slopes, qx, kx, vt)


def _shift_rows(x, d, fill):
    n, c = x.shape
    if d % HALO_ROWS == 0:
        return jnp.concatenate([jnp.full((d, c), fill, x.dtype), x[:n - d]], axis=0)
    rolled = pltpu.roll(x, d, axis=0)
    rows = lax.broadcasted_iota(jnp.int32, x.shape, 0)
    return jnp.where(rows < d, fill, rolled)


def _rglru_kernel(x_ref, xh_ref, gr_ref, cw_ref, cb_ref, w_ref, ba_ref, bx_ref, lam_ref, o_ref,
                  h_scr, *, ts, n_blocks):
    si = pl.program_id(1)

    @pl.when(si == 0)
    def _():
        h_scr[...] = jnp.zeros_like(h_scr)

    xb = x_ref[...]
    cw = cw_ref[...]
    halo = jnp.where(si == 0, 0.0, xh_ref[...].astype(F32))
    halo_row = lax.broadcasted_iota(jnp.int32, halo.shape, 0)
    delta = (lax.broadcasted_iota(jnp.int32, (ts, ts), 0) - lax.broadcasted_iota(jnp.int32, (ts, ts), 1))
    xc = cw[C_CONV - 1:C_CONV, :] * xb.astype(F32) + cb_ref[...]
    head = jnp.zeros_like(halo)
    for j in range(C_CONV - 1):
        d = C_CONV - 1 - j
        shifted = _dot(jnp.where(delta == d, 1.0, 0.0).astype(BF16), xb)
        xc += cw[j:j + 1, :] * shifted
        head += cw[j:j + 1, :] * jnp.where(halo_row < d, pltpu.roll(halo, d, axis=0), 0.0)
    xc = jnp.concatenate([xc[:HALO_ROWS] + head, xc[HALO_ROWS:]], axis=0)

    bd = xc.shape[1] // n_blocks
    xcb = xc.astype(BF16)
    gates = [_dot(xcb[:, g * bd:(g + 1) * bd], w_ref[g]) for g in range(n_blocks)]
    r = jax.nn.sigmoid(jnp.concatenate([gt[:, :bd] for gt in gates], axis=1) + ba_ref[...])
    i = jax.nn.sigmoid(jnp.concatenate([gt[:, bd:] for gt in gates], axis=1) + bx_ref[...])
    log_a = (-RG_C) * r * jax.nn.softplus(-lam_ref[...])
    a = jnp.exp(log_a)
    z = -jnp.tanh(log_a) * (a * a + 1.0)
    u = jnp.where(z > 0.0, z * lax.rsqrt(z), 0.0) * (i * xc)

    d = 1
    while d < ts:
        u = a * _shift_rows(u, d, 0.0) + u
        a = a * _shift_rows(a, d, 1.0)
        d *= 2
    hs = a * h_scr[...] + u
    h_scr[...] = hs[ts - 1:ts, :]
    o_ref[...] = (hs * jax.nn.gelu(gr_ref[...].astype(F32))).astype(o_ref.dtype)


def _rglru(proj3, conv_w, conv_b, w_cat, ba, bx, lam, *, ts=256):
    bsz, s, _ = proj3.shape
    c = conv_w.shape[1]
    n_blocks, bd, _ = w_cat.shape
    hb = ts // HALO_ROWS
    vec = pl.BlockSpec((1, c), lambda b, i: (0, 0))
    return pl.pallas_call(
        functools.partial(_rglru_kernel, ts=ts, n_blocks=n_blocks),
        grid=(bsz, s // ts),
        in_specs=[
            pl.BlockSpec((None, ts, c), lambda b, i: (b, i, 0)),
            pl.BlockSpec((None, HALO_ROWS, c), lambda b, i: (b, jnp.maximum(i * hb - 1, 0), 0)),
            pl.BlockSpec((None, ts, c), lambda b, i: (b, i, 1)),
            pl.BlockSpec((C_CONV, c), lambda b, i: (0, 0)),
            vec,
            pl.BlockSpec((n_blocks, bd, 2 * bd), lambda b, i: (0, 0, 0)),
            vec, vec, vec,
        ],
        out_specs=pl.BlockSpec((None, ts, c), lambda b, i: (b, i, 0)),
        out_shape=jax.ShapeDtypeStruct((bsz, s, c), BF16),
        scratch_shapes=[pltpu.VMEM((1, c), F32)],
        compiler_params=_cparams("parallel", "arbitrary"),
        name="rglru_scan",
    )(proj3, proj3, proj3, conv_w, conv_b.reshape(1, c), w_cat, ba.reshape(1, c), bx.reshape(1, c),
      lam.reshape(1, c))


def _mlstm_kernel(q_ref, k_ref, v_ref, og_ref, gt_ref, o_ref, c_scr, n_scr, m_scr, *, chunk, nh, dh):
    ci = pl.program_id(1)

    @pl.when(ci == 0)
    def _():
        c_scr[...] = jnp.zeros_like(c_scr)
        n_scr[...] = jnp.zeros_like(n_scr)
        m_scr[...] = jnp.zeros_like(m_scr)

    gates = gt_ref[...]
    bcum = jax.nn.log_sigmoid(gates)
    d = 1
    while d < chunk:
        bcum = bcum + _shift_rows(bcum, d, 0.0)
        d *= 2
    lane = lax.broadcasted_iota(jnp.int32, gates.shape, 1)
    rows_t = jnp.where(lane < nh, gates, bcum).T
    r_idx = lax.broadcasted_iota(jnp.int32, (chunk, chunk), 0)
    c_idx = lax.broadcasted_iota(jnp.int32, (chunk, chunk), 1)
    causal = c_idx <= r_idx
    k_scale = dh ** -0.5

    head_cols = [slice(h * dh, (h + 1) * dh) for h in range(nh)]
    qs = [q_ref[:, cols] for cols in head_cols]
    kss = [k_ref[:, cols] * jnp.asarray(k_scale, BF16) for cols in head_cols]
    vs = [v_ref[:, cols] for cols in head_cols]
    c_prevs = [c_scr[h] for h in range(nh)]
    qk = [_dot_nt(qs[h], kss[h]) for h in range(nh)]
    qc = [_dot(qs[h], c_prevs[h].astype(BF16)) for h in range(nh)]

    stage = []
    for h in range(nh):
        i_col = gates[:, h:h + 1]
        b_col = bcum[:, nh + h:nh + h + 1]
        i_row = rows_t[h:h + 1, :]
        b_row = rows_t[nh + h:nh + h + 1, :]
        m_prev = m_scr[h]

        dmat = jnp.where(causal, b_col - b_row + i_row, NEG_INF)
        inter = b_col + m_prev
        m_t = jnp.maximum(inter, jnp.max(dmat, axis=-1, keepdims=True))
        w_inter = jnp.exp(inter - m_t)
        s_qk = qk[h] * jnp.exp(dmat - m_t)
        n_prev = n_scr[h]
        q_n = jnp.sum(qs[h].astype(F32) * n_prev, axis=-1, keepdims=True)
        den = w_inter * q_n + jnp.sum(s_qk, axis=-1, keepdims=True)

        b_last = b_col[chunk - 1:chunk, :]
        w_s = b_last - b_col + i_col
        m_new = jnp.maximum(b_last + m_prev, jnp.max(w_s, axis=0, keepdims=True))
        decay = jnp.exp(b_last + m_prev - m_new)
        kw = jnp.exp(w_s - m_new) * kss[h].astype(F32)
        n_scr[h] = decay * n_prev + jnp.sum(kw, axis=0, keepdims=True)
        m_scr[h] = m_new
        stage.append((s_qk.astype(BF16), kw.astype(BF16), w_inter, den, m_t, decay))

    sv = [_dot(stage[h][0], vs[h]) for h in range(nh)]
    kv = [_dot_tn(stage[h][1], vs[h]) for h in range(nh)]
    for h in range(nh):
        _, _, w_inter, den, m_t, decay = stage[h]
        hid = (w_inter * qc[h] + sv[h]) / jnp.maximum(jnp.abs(den), jnp.exp(-m_t))
        c_scr[h] = decay * c_prevs[h] + kv[h]
        cols = head_cols[h]
        o_ref[:, cols] = (hid * jax.nn.sigmoid(og_ref[:, cols].astype(F32))).astype(o_ref.dtype)


def _mlstm(proj3, gates3, *, col0, nh, chunk=MLSTM_CHUNK):
    bsz, s, _ = proj3.shape
    dh = 256
    w = nh * dh
    cb = col0 // w
    blk = lambda k: pl.BlockSpec((None, chunk, w), lambda b, i: (b, i, cb + k))
    return pl.pallas_call(
        functools.partial(_mlstm_kernel, chunk=chunk, nh=nh, dh=dh),
        grid=(bsz, s // chunk),
        in_specs=[blk(0), blk(1), blk(2), blk(3),
                  pl.BlockSpec((None, chunk, GATE_LANES), lambda b, i: (b, i, 0))],
        out_specs=pl.BlockSpec((None, chunk, w), lambda b, i: (b, i, 0)),
        out_shape=jax.ShapeDtypeStruct((bsz, s, w), BF16),
        scratch_shapes=[pltpu.VMEM((nh, dh, dh), F32), pltpu.VMEM((nh, 1, dh), F32),
                        pltpu.VMEM((nh, 1, 1), F32)],
        compiler_params=_cparams("parallel", "arbitrary"),
        name="mlstm_chunkwise",
    )(proj3, proj3, proj3, proj3, gates3)


def kernel(x, c, mod_w, mod_b, norm_mix_g, norm_ffn_g, ffn_w1, ffn_w3, ffn_w2, ab_w_in, ab_conv_w, ab_w_out, cd_w_in, cd_gate_b, rg_conv_w, rg_conv_b, rg_wa, rg_ba, rg_wx, rg_bx, rg_lambda, cd_w_out, final_norm_g):
    bsz, seq, d = x.shape
    depth = mod_w.shape[0]
    t = bsz * seq
    half = d // 2
    a_heads = half // A_HEAD_DIM
    slopes = 2.0 ** (-8.0 * jnp.arange(1, a_heads + 1, dtype=F32) / a_heads)

    mod = _modulation(c, mod_w, mod_b)
    ffn_w1_q, ffn_s1 = _fp8_weights(ffn_w1)
    ffn_w3_q, ffn_s3 = _fp8_weights(ffn_w3)
    ffn_w2_b = ffn_w2.astype(BF16)
    ab_w_in_b, ab_w_out_b = ab_w_in.astype(BF16), ab_w_out.astype(BF16)
    cd_w_in_b, cd_w_out_b = cd_w_in.astype(BF16), cd_w_out.astype(BF16)
    x2 = x.reshape(t, d)
    for layer in range(depth):
        mod_l = mod[layer]
        if layer % 2 == 0:
            e = layer // 2
            n_main = 6 * half
            proj = _in_proj(x2, norm_mix_g[layer], mod_l, ab_w_in_b, e, n_main, seq, shift_row=0, scale_row=1)
            proj3 = proj.reshape(bsz, seq, n_main)
            y_a = _moba(proj3, slopes, a_heads).reshape(t, half)
            y_b, conv = proj, dict(conv_w=ab_conv_w[e], conv_col0=3 * half)
            w_out, w_idx = ab_w_out_b, e
        else:
            o = layer // 2
            n_main = 2 * half + 4 * half
            n_g = cd_w_in.shape[2] - n_main
            w_gate = jnp.zeros((d, GATE_LANES), BF16).at[:, :n_g].set(cd_w_in[o, :, n_main:].astype(BF16))
            b_gate = jnp.zeros((1, GATE_LANES), F32).at[0, :n_g].set(cd_gate_b[o])
            proj, gates = _in_proj(x2, norm_mix_g[layer], mod_l, cd_w_in_b, o, n_main, seq,
                                   shift_row=0, scale_row=1, w_gate=w_gate, b_gate=b_gate)
            proj3 = proj.reshape(bsz, seq, n_main)
            w_cat = jnp.concatenate([rg_wa[o], rg_wx[o]], axis=-1).astype(BF16)
            y_a = _rglru(proj3, rg_conv_w[o], rg_conv_b[o], w_cat, rg_ba[o], rg_bx[o],
                         rg_lambda[o]).reshape(t, half)
            y_b = _mlstm(proj3, gates.reshape(bsz, seq, GATE_LANES), col0=2 * half,
                         nh=D_HEADS).reshape(t, half)
            w_out, w_idx, conv = cd_w_out_b, o, {}
        x2 = _out_proj(y_a, y_b, w_out, w_idx, x2, mod_l, seq, gate_row=2, **conv)
        final_g = final_norm_g if layer == depth - 1 else None
        x2 = _ffn(x2, norm_ffn_g[layer], mod_l, ffn_w1_q, ffn_w3_q, ffn_s1, ffn_s3, ffn_w2_b,
                  layer, seq, final_g)
    return x2.reshape(bsz, seq, d)
```

```python
import functools

import jax
import jax.numpy as jnp
from jax import lax
from jax.experimental import pallas as pl
from jax.experimental.pallas import tpu as pltpu

F32 = jnp.float32
BF16 = jnp.bfloat16
F8 = jnp.float8_e4m3fn
F8_MAX = 448.0
F8_TINY = 1e-30

EPS = 1e-6
NEG_INF = -1e30
LOG2E = 1.4426950408889634

A_HEAD_DIM = 128
MOBA_BLOCK = 256
MOBA_TOPK = 3
MOBA_GROUP = 4
RANK_QBLOCKS = 8
VT_PAD_ROWS = 16
B_CONV = 3
C_BLOCKS = 8
C_CONV = 4
RG_C = 8.0
D_HEADS = 4
MLSTM_CHUNK = 256
GATE_LANES = 128

VMEM_LIMIT_BYTES = 60 * 1024 * 1024
HALO_ROWS = 8
NORM_CHUNKS = 4


def _cparams(*sem):
    return pltpu.CompilerParams(dimension_semantics=sem, vmem_limit_bytes=VMEM_LIMIT_BYTES)


def _dot(a, b):
    return jnp.dot(a, b, preferred_element_type=F32)


def _dot_nt(a, b):
    return lax.dot_general(a, b, (((1,), (1,)), ((), ())), preferred_element_type=F32)


def _dot_tn(a, b):
    return lax.dot_general(a, b, (((0,), (0,)), ((), ())), preferred_element_type=F32)


def _norm_mod(x, g, scale, shift):
    xn = x * lax.rsqrt(jnp.mean(x * x, axis=-1, keepdims=True) + EPS)
    return (xn * g) * (1.0 + scale) + shift


def _mod_kernel(c_ref, w_ref, b_ref, o_ref):
    c = c_ref[...]
    cond = c * jax.nn.sigmoid(c)
    o_ref[...] = _dot(cond.astype(BF16), w_ref[...].astype(BF16)) + b_ref[...]


def _modulation(c, mod_w, mod_b):
    depth, d, n = mod_w.shape
    bsz = c.shape[0]
    rows = 8
    c8 = jnp.zeros((rows, d), F32).at[:bsz].set(c)
    tn = 1024
    out = pl.pallas_call(
        _mod_kernel,
        grid=(depth, n // tn),
        in_specs=[
            pl.BlockSpec((rows, d), lambda l, j: (0, 0)),
            pl.BlockSpec((None, d, tn), lambda l, j: (l, 0, j)),
            pl.BlockSpec((None, 1, tn), lambda l, j: (l, 0, j)),
        ],
        out_specs=pl.BlockSpec((None, rows, tn), lambda l, j: (l, 0, j)),
        out_shape=jax.ShapeDtypeStruct((depth, rows, n), F32),
        compiler_params=_cparams("arbitrary", "arbitrary"),
        name="adaln_modulation",
    )(c8, mod_w, mod_b.reshape(depth, 1, n))
    return out[:, :bsz].reshape(depth, bsz, 6, d)


def _in_proj_kernel(x_ref, g_ref, mod_ref, w_ref, *rest, shift_row, scale_row, with_gates):
    if with_gates:
        wg_ref, bg_ref, o_ref, og_ref, h_scr = rest
    else:
        o_ref, h_scr = rest
    j = pl.program_id(1)

    @pl.when(j == 0)
    def _():
        rc = x_ref.shape[0] // NORM_CHUNKS
        for r in range(NORM_CHUNKS):
            rows = slice(r * rc, (r + 1) * rc)
            h = _norm_mod(x_ref[rows, :], g_ref[...], mod_ref[scale_row:scale_row + 1, :],
                          mod_ref[shift_row:shift_row + 1, :])
            hb = h.astype(BF16)
            h_scr[rows, :] = hb
            o_ref[rows, :] = _dot(hb, w_ref[...]).astype(o_ref.dtype)
            if with_gates:
                og_ref[rows, :] = _dot(hb, wg_ref[...]) + bg_ref[...]

    @pl.when(j > 0)
    def _():
        o_ref[...] = _dot(h_scr[...], w_ref[...]).astype(o_ref.dtype)


def _in_proj(x2, g, mod_l, w, idx, n, seq, *, shift_row, scale_row, w_gate=None, b_gate=None, tm=1024, tn=2048):
    t, d = x2.shape
    tiles_per_seq = seq // tm
    with_gates = w_gate is not None
    in_specs = [
        pl.BlockSpec((tm, d), lambda i, j: (i, 0)),
        pl.BlockSpec((1, d), lambda i, j: (0, 0)),
        pl.BlockSpec((None, 6, d), lambda i, j: (i // tiles_per_seq, 0, 0)),
        pl.BlockSpec((None, d, tn), lambda i, j: (idx, 0, j)),
    ]
    args = [x2, g.reshape(1, d), mod_l, w]
    out_specs = pl.BlockSpec((tm, tn), lambda i, j: (i, j))
    out_shape = jax.ShapeDtypeStruct((t, n), BF16)
    if with_gates:
        in_specs += [pl.BlockSpec((d, GATE_LANES), lambda i, j: (0, 0)),
                     pl.BlockSpec((1, GATE_LANES), lambda i, j: (0, 0))]
        args += [w_gate, b_gate]
        out_specs = [out_specs, pl.BlockSpec((tm, GATE_LANES), lambda i, j: (i, 0))]
        out_shape = [out_shape, jax.ShapeDtypeStruct((t, GATE_LANES), F32)]
    return pl.pallas_call(
        functools.partial(_in_proj_kernel, shift_row=shift_row, scale_row=scale_row, with_gates=with_gates),
        grid=(t // tm, n // tn),
        in_specs=in_specs,
        out_specs=out_specs,
        out_shape=out_shape,
        scratch_shapes=[pltpu.VMEM((tm, d), BF16)],
        compiler_params=_cparams("parallel", "arbitrary"),
        name="norm_in_proj_gates" if with_gates else "norm_in_proj",
    )(*args)


def _gated_conv_tile(gb_ref, gc_ref, xb_ref, gch_ref, xbh_ref, w_ref, u_scr, first):
    tm = gc_ref.shape[0]
    u_scr[0:HALO_ROWS, :] = jnp.where(first, 0.0, gch_ref[...].astype(F32) * xbh_ref[...].astype(F32))
    u = gc_ref[...].astype(F32) * xb_ref[...].astype(F32)
    u_scr[HALO_ROWS:, :] = u
    w = w_ref[...]
    conv = w[B_CONV - 1:B_CONV, :] * u
    for j in range(B_CONV - 1):
        off = HALO_ROWS - (B_CONV - 1) + j
        conv += w[j:j + 1, :] * u_scr[off:off + tm, :]
    return gb_ref[...].astype(F32) * conv


def _out_proj_kernel(a_ref, *rest, gate_row, half, conv_tiles_per_seq):
    if conv_tiles_per_seq:
        gb_ref, gc_ref, xb_ref, gch_ref, xbh_ref, cw_ref, w_ref, x_ref, mod_ref, o_ref, u_scr = rest
    else:
        b_ref, w_ref, x_ref, mod_ref, o_ref = rest
    acc = _dot(a_ref[...], w_ref[0:half, :])
    if conv_tiles_per_seq:
        first = (pl.program_id(0) % conv_tiles_per_seq) == 0
        b = _gated_conv_tile(gb_ref, gc_ref, xb_ref, gch_ref, xbh_ref, cw_ref, u_scr, first).astype(BF16)
    else:
        b = b_ref[...]
    acc += _dot(b, w_ref[half:, :])
    o_ref[...] = x_ref[...] + mod_ref[gate_row:gate_row + 1, :] * acc


def _out_proj(a, b, w, idx, x2, mod_l, seq, *, gate_row, conv_w=None, conv_col0=None, tm=512):
    t, d = x2.shape
    half = a.shape[1]
    tiles_per_seq = seq // tm
    fused_conv = conv_w is not None
    row_block = pl.BlockSpec((tm, half), lambda i: (i, 0))
    if fused_conv:
        cb = conv_col0 // half
        hb = tm // HALO_ROWS
        halo = lambda k: pl.BlockSpec((HALO_ROWS, half), lambda i: (jnp.maximum(i * hb - 1, 0), cb + k))
        b_specs = [pl.BlockSpec((tm, half), lambda i: (i, cb)), pl.BlockSpec((tm, half), lambda i: (i, cb + 1)),
                   pl.BlockSpec((tm, half), lambda i: (i, cb + 2)), halo(1), halo(2),
                   pl.BlockSpec((B_CONV, half), lambda i: (0, 0))]
        b_args = [b, b, b, b, b, conv_w]
        scratch = [pltpu.VMEM((tm + HALO_ROWS, half), F32)]
    else:
        b_specs, b_args, scratch = [row_block], [b], []
    return pl.pallas_call(
        functools.partial(_out_proj_kernel, gate_row=gate_row, half=half,
                          conv_tiles_per_seq=tiles_per_seq if fused_conv else 0),
        grid=(t // tm,),
        in_specs=[row_block] + b_specs + [
            pl.BlockSpec((None, 2 * half, d), lambda i: (idx, 0, 0)),
            pl.BlockSpec((tm, d), lambda i: (i, 0)),
            pl.BlockSpec((None, 6, d), lambda i: (i // tiles_per_seq, 0, 0)),
        ],
        out_specs=pl.BlockSpec((tm, d), lambda i: (i, 0)),
        out_shape=jax.ShapeDtypeStruct((t, d), F32),
        scratch_shapes=scratch,
        compiler_params=_cparams("parallel"),
        name="out_proj_conv_residual" if fused_conv else "out_proj_residual",
    )(a, *b_args, w, x2, mod_l)


def _ffn_up_kernel(x_ref, g_ref, mod_ref, w1_ref, w3_ref, s1_ref, s3_ref, o_ref, h_scr, hs_scr):
    j = pl.program_id(1)
    tm = x_ref.shape[0]

    def gated(h8, row_scale):
        a = _dot(h8, w1_ref[...]) * row_scale * s1_ref[...]
        b = _dot(h8, w3_ref[...]) * row_scale * s3_ref[...]
        return ((a * jax.nn.sigmoid(a)) * b).astype(o_ref.dtype)

    @pl.when(j == 0)
    def _():
        rc = tm // NORM_CHUNKS
        for r in range(NORM_CHUNKS):
            rows = slice(r * rc, (r + 1) * rc)
            h = _norm_mod(x_ref[rows, :], g_ref[...], mod_ref[4:5, :], mod_ref[3:4, :])
            amax = jnp.maximum(jnp.max(jnp.abs(h), axis=-1, keepdims=True), F8_TINY)
            h8 = (h * (F8_MAX / amax)).astype(F8)
            row_scale = amax * (1.0 / F8_MAX)
            h_scr[rows, :] = h8
            hs_scr[rows, :] = row_scale
            o_ref[rows, :] = gated(h8, row_scale)

    @pl.when(j > 0)
    def _():
        o_ref[...] = gated(h_scr[...], hs_scr[...])


def _ffn_down_kernel(u_ref, w2_ref, x_ref, mod_ref, *rest, final):
    if final:
        fg_ref, o_ref = rest
    else:
        (o_ref,) = rest
    y = x_ref[...] + mod_ref[5:6, :] * _dot(u_ref[...], w2_ref[...])
    if final:
        y = y * lax.rsqrt(jnp.mean(y * y, axis=-1, keepdims=True) + EPS) * fg_ref[...]
    o_ref[...] = y


def _fp8_weights_kernel(w_ref, q_ref, s_ref):
    w = w_ref[...]
    amax = jnp.maximum(jnp.max(jnp.abs(w), axis=0, keepdims=True), F8_TINY)
    q_ref[...] = (w * (F8_MAX / amax)).astype(q_ref.dtype)
    s_ref[...] = amax * (1.0 / F8_MAX)


def _fp8_weights(w, *, tn=512):
    depth, k, n = w.shape
    return pl.pallas_call(
        _fp8_weights_kernel,
        grid=(depth, n // tn),
        in_specs=[pl.BlockSpec((None, k, tn), lambda l, j: (l, 0, j))],
        out_specs=[pl.BlockSpec((None, k, tn), lambda l, j: (l, 0, j)),
                   pl.BlockSpec((None, 1, tn), lambda l, j: (l, 0, j))],
        out_shape=[jax.ShapeDtypeStruct((depth, k, n), F8), jax.ShapeDtypeStruct((depth, 1, n), F32)],
        compiler_params=_cparams("parallel", "parallel"),
        name="fp8_weight_prep",
    )(w)


def _ffn(x2, g, mod_l, w1, w3, s1, s3, w2, layer, seq, final_g=None, *, tm=1024, tf=512, tm_down=512):
    t, d = x2.shape
    f = w1.shape[2]
    tiles_per_seq = seq // tm
    hidden = pl.pallas_call(
        _ffn_up_kernel,
        grid=(t // tm, f // tf),
        in_specs=[
            pl.BlockSpec((tm, d), lambda i, j: (i, 0)),
            pl.BlockSpec((1, d), lambda i, j: (0, 0)),
            pl.BlockSpec((None, 6, d), lambda i, j: (i // tiles_per_seq, 0, 0)),
            pl.BlockSpec((None, d, tf), lambda i, j: (layer, 0, j)),
            pl.BlockSpec((None, d, tf), lambda i, j: (layer, 0, j)),
            pl.BlockSpec((None, 1, tf), lambda i, j: (layer, 0, j)),
            pl.BlockSpec((None, 1, tf), lambda i, j: (layer, 0, j)),
        ],
        out_specs=pl.BlockSpec((tm, tf), lambda i, j: (i, j)),
        out_shape=jax.ShapeDtypeStruct((t, f), BF16),
        scratch_shapes=[pltpu.VMEM((tm, d), F8), pltpu.VMEM((tm, 1), F32)],
        compiler_params=_cparams("parallel", "arbitrary"),
        name="swiglu_up",
    )(x2, g.reshape(1, d), mod_l, w1, w3, s1, s3)

    final = final_g is not None
    down_tiles_per_seq = seq // tm_down
    in_specs = [
        pl.BlockSpec((tm_down, f), lambda i: (i, 0)),
        pl.BlockSpec((None, f, d), lambda i: (layer, 0, 0), pipeline_mode=pl.Buffered(1)),
        pl.BlockSpec((tm_down, d), lambda i: (i, 0)),
        pl.BlockSpec((None, 6, d), lambda i: (i // down_tiles_per_seq, 0, 0)),
    ]
    args = [hidden, w2, x2, mod_l]
    if final:
        in_specs.append(pl.BlockSpec((1, d), lambda i: (0, 0)))
        args.append(final_g.reshape(1, d))
    return pl.pallas_call(
        functools.partial(_ffn_down_kernel, final=final),
        grid=(t // tm_down,),
        in_specs=in_specs,
        out_specs=pl.BlockSpec((tm_down, d), lambda i: (i, 0)),
        out_shape=jax.ShapeDtypeStruct((t, d), F32),
        compiler_params=_cparams("parallel"),
        name="swiglu_down_final" if final else "swiglu_down",
    )(*args)


def _split3(x):
    hi = x.astype(BF16).astype(F32)
    mid = (x - hi).astype(BF16).astype(F32)
    lo = (x - hi - mid).astype(BF16).astype(F32)
    return hi, mid, lo


def _moba_prep_kernel(slopes_ref, q_ref, k_ref, v_ref, qx_ref, kx_ref, vt_ref, *, nb, blk, grp):
    s, dh = q_ref.shape
    u = slopes_ref[pl.program_id(1)] * (dh ** 0.5)
    kmean = jnp.mean(k_ref[...].astype(F32).reshape(nb, blk, dh), axis=1)
    hi = kmean.astype(BF16)
    lo = (kmean - hi.astype(F32)).astype(BF16)
    q = q_ref[...]
    gate = _dot_nt(hi, q) + _dot_nt(lo, q)
    lanes_per = RANK_QBLOCKS * blk
    pieces = []
    for c in range(s // lanes_per):
        nr = min(nb, RANK_QBLOCKS * (c + 1))
        n_idx = lax.broadcasted_iota(jnp.int32, (nr, lanes_per), 0)
        q_blk = lax.broadcasted_iota(jnp.int32, (nr, lanes_per), 1) // blk + c * RANK_QBLOCKS
        eligible = n_idx < q_blk
        g = jnp.where(eligible, gate[0:nr, c * lanes_per:(c + 1) * lanes_per], NEG_INF)
        rank = jnp.zeros((nr, lanes_per), jnp.int32)
        for m in range(nr):
            gm = g[m:m + 1, :]
            beats = (gm > g) | ((gm == g) & (n_idx > m))
            rank += beats.astype(jnp.int32)
        keep = (eligible & (rank < MOBA_TOPK)) | (n_idx == q_blk)
        piece = jnp.where(keep, 0.0, NEG_INF)
        if nr < nb:
            piece = jnp.concatenate([piece, jnp.full((nb - nr, lanes_per), NEG_INF, F32)], axis=0)
        pieces.append(piece)
    sel_bias = jnp.concatenate(pieces, axis=1) if len(pieces) > 1 else pieces[0]

    off = (lax.broadcasted_iota(jnp.int32, (8, s), 1) % blk).astype(F32) * u
    hi, mid, lo = _split3(off)
    feat = lax.broadcasted_iota(jnp.int32, (8, s), 0)
    pos_rows = jnp.where(feat == 0, hi, jnp.where(feat == 1, mid, jnp.where(feat == 2, lo,
                         jnp.where(feat < 6, 1.0, 0.0))))
    ext_t = jnp.concatenate([sel_bias, pos_rows, jnp.zeros((dh - nb - 8, s), F32)], axis=0)
    chunk = grp * blk
    for ci in range(s // chunk):
        rows = slice(ci * chunk, (ci + 1) * chunk)
        qx_ref[rows, 0:dh] = q_ref[rows, :]
        qx_ref[rows, dh:2 * dh] = ext_t[:, rows].T.astype(qx_ref.dtype)

    col = lax.broadcasted_iota(jnp.int32, (blk, dh), 1)
    khi, kmid, klo = _split3(lax.broadcasted_iota(jnp.int32, (blk, dh), 0).astype(F32) * u)
    tile = jnp.where((col >= nb) & (col < nb + 3), -1.0,
                     jnp.where(col == nb + 3, khi, jnp.where(col == nb + 4, kmid,
                               jnp.where(col == nb + 5, klo, 0.0))))
    kx_ref[:, 0:dh] = k_ref[...]
    for n in range(nb):
        kx_ref[n * blk:(n + 1) * blk, dh:2 * dh] = jnp.where(col == n, 1.0, tile).astype(kx_ref.dtype)

    pad_row = lax.broadcasted_iota(jnp.int32, (VT_PAD_ROWS, blk), 0)
    ones_pad = jnp.where(pad_row == 0, 1.0, 0.0).astype(vt_ref.dtype)
    for n in range(nb):
        v_t = v_ref[n * blk:(n + 1) * blk, :].astype(F32).T.astype(vt_ref.dtype)
        cols = slice((n % grp) * blk, (n % grp + 1) * blk)
        vt_ref[n // grp, 0:dh, cols] = v_t
        vt_ref[n // grp, dh:dh + VT_PAD_ROWS, cols] = ones_pad


def _moba_attn_kernel(slopes_ref, qx_ref, kx_ref, vt_ref, o_ref, m_scr, acc_scr, raw_a, raw_b, *,
                      blk, grp, scale):
    h = pl.program_id(1)
    qt = pl.program_id(2)
    slope2 = slopes_ref[h] * LOG2E
    c2 = scale * LOG2E
    causal = (lax.broadcasted_iota(jnp.int32, (blk, blk), 0)
              <= lax.broadcasted_iota(jnp.int32, (blk, blk), 1))

    def scores(b, kg):
        return _dot_nt(kg, qx_ref[b * blk:(b + 1) * blk, :])

    def update(b, tile, vt_g, n0, n_sub, first):
        q_blk = qt * grp + b
        shifts = []
        m_new = None if first else m_scr[b]
        for gi in range(n_sub):
            shift = -slope2 * (blk * (q_blk - (n0 + gi))).astype(F32)
            cm = jnp.max(tile(gi), axis=0, keepdims=True) * c2 + shift
            m_new = cm if m_new is None else jnp.maximum(m_new, cm)
            shifts.append(shift)
        probs = [jnp.exp2(tile(gi) * c2 - (m_new - shift)).astype(BF16) for gi, shift in enumerate(shifts)]
        pv = _dot(vt_g, jnp.concatenate(probs, axis=0) if n_sub > 1 else probs[0])
        if first:
            acc_scr[b] = pv
        else:
            acc_scr[b] = jnp.exp2(m_scr[b] - m_new) * acc_scr[b] + pv
        m_scr[b] = m_new


    def issue(g, buf):
        kg = kx_ref[pl.ds(pl.multiple_of(g * (grp * blk), grp * blk), grp * blk), :]
        for b in range(grp):
            buf[b] = scores(b, kg)

    def consume(g, buf):
        vt_g = vt_ref[g]
        for b in range(grp):
            update(b, lambda gi, b=b: buf[b, gi * blk:(gi + 1) * blk, :], vt_g, g * grp, grp, False)

    n_full = (qt - 1) // 2
    rest = qt - 2 * n_full

    base = pl.multiple_of(qt * (grp * blk), grp * blk)
    raws = [scores(b, kx_ref[pl.ds(base, (b + 1) * blk), :]) for b in range(grp)]
    issue(0, raw_a)
    for b in range(grp):
        def diag_tile(gi, b=b):
            r = raws[b][gi * blk:(gi + 1) * blk, :]
            return jnp.where(causal, r, NEG_INF) if gi == b else r
        update(b, diag_tile, vt_ref[qt, :, 0:(b + 1) * blk], qt * grp, b + 1, True)

    def body(i, carry):
        issue(2 * i + 1, raw_b)
        consume(2 * i, raw_a)
        issue(2 * i + 2, raw_a)
        consume(2 * i + 1, raw_b)
        return carry

    lax.fori_loop(0, jnp.maximum(n_full, 0), body, 0)

    @pl.when((qt > 0) & (rest == 1))
    def _():
        consume(2 * n_full, raw_a)

    @pl.when((qt > 0) & (rest == 2))
    def _():
        issue(2 * n_full + 1, raw_b)
        consume(2 * n_full, raw_a)
        consume(2 * n_full + 1, raw_b)

    dh = o_ref.shape[1]
    for b in range(grp):
        acc = acc_scr[b]
        o_ref[b * blk:(b + 1) * blk, :] = (acc[0:dh, :] / acc[dh:dh + 1, :]).T.astype(o_ref.dtype)


def _moba(proj3, slopes, n_heads):
    bsz, s, _ = proj3.shape
    dh, blk, grp = A_HEAD_DIM, MOBA_BLOCK, MOBA_GROUP
    nb = s // blk
    ng = nb // grp
    tq = grp * blk
    smem = pl.BlockSpec(memory_space=pltpu.SMEM)
    qx, kx, vt = pl.pallas_call(
        functools.partial(_moba_prep_kernel, nb=nb, blk=blk, grp=grp),
        grid=(bsz, n_heads),
        in_specs=[
            smem,
            pl.BlockSpec((None, s, dh), lambda b, h: (b, 0, h)),
            pl.BlockSpec((None, s, dh), lambda b, h: (b, 0, n_heads + h)),
            pl.BlockSpec((None, s, dh), lambda b, h: (b, 0, 2 * n_heads + h)),
        ],
        out_specs=[
            pl.BlockSpec((None, None, s, 2 * dh), lambda b, h: (b, h, 0, 0)),
            pl.BlockSpec((None, None, s, 2 * dh), lambda b, h: (b, h, 0, 0)),
            pl.BlockSpec((None, None, ng, dh + VT_PAD_ROWS, tq), lambda b, h: (b, h, 0, 0, 0)),
        ],
        out_shape=[
            jax.ShapeDtypeStruct((bsz, n_heads, s, 2 * dh), BF16),
            jax.ShapeDtypeStruct((bsz, n_heads, s, 2 * dh), BF16),
            jax.ShapeDtypeStruct((bsz, n_heads, ng, dh + VT_PAD_ROWS, tq), BF16),
        ],
        compiler_params=_cparams("parallel", "parallel"),
        name="moba_select",
    )(slopes, proj3, proj3, proj3)
    return pl.pallas_call(
        functools.partial(_moba_attn_kernel, blk=blk, grp=grp, scale=dh ** -0.5),
        grid=(bsz, n_heads, ng),
        in_specs=[
            smem,
            pl.BlockSpec((None, None, tq, 2 * dh), lambda b, h, i: (b, h, i, 0)),
            pl.BlockSpec((None, None, s, 2 * dh), lambda b, h, i: (b, h, 0, 0)),
            pl.BlockSpec((None, None, ng, dh + VT_PAD_ROWS, tq), lambda b, h, i: (b, h, 0, 0, 0)),
        ],
        out_specs=pl.BlockSpec((None, tq, dh), lambda b, h, i: (b, i, h)),
        out_shape=jax.ShapeDtypeStruct((bsz, s, n_heads * dh), BF16),
        scratch_shapes=[pltpu.VMEM((grp, 1, blk), F32), pltpu.VMEM((grp, dh + VT_PAD_ROWS, blk), F32),
                        pltpu.VMEM((grp, tq, blk), F32), pltpu.VMEM((grp, tq, blk), F32)],
        compiler_params=_cparams("parallel", "parallel", "arbitrary"),
        name="moba_attention",
    )(slopes, qx, kx, vt)


def _shift_rows(x, d, fill):
    n, c = x.shape
    if d % HALO_ROWS == 0:
        return jnp.concatenate([jnp.full((d, c), fill, x.dtype), x[:n - d]], axis=0)
    rolled = pltpu.roll(x, d, axis=0)
    rows = lax.broadcasted_iota(jnp.int32, x.shape, 0)
    return jnp.where(rows < d, fill, rolled)


def _rglru_kernel(x_ref, xh_ref, gr_ref, cw_ref, cb_ref, w_ref, ba_ref, bx_ref, lam_ref, o_ref,
                  h_scr, *, ts, n_blocks):
    si = pl.program_id(1)

    @pl.when(si == 0)
    def _():
        h_scr[...] = jnp.zeros_like(h_scr)

    xb = x_ref[...]
    cw = cw_ref[...]
    halo = jnp.where(si == 0, 0.0, xh_ref[...].astype(F32))
    halo_row = lax.broadcasted_iota(jnp.int32, halo.shape, 0)
    delta = (lax.broadcasted_iota(jnp.int32, (ts, ts), 0) - lax.broadcasted_iota(jnp.int32, (ts, ts), 1))
    xc = cw[C_CONV - 1:C_CONV, :] * xb.astype(F32) + cb_ref[...]
    head = jnp.zeros_like(halo)
    for j in range(C_CONV - 1):
        d = C_CONV - 1 - j
        shifted = _dot(jnp.where(delta == d, 1.0, 0.0).astype(BF16), xb)
        xc += cw[j:j + 1, :] * shifted
        head += cw[j:j + 1, :] * jnp.where(halo_row < d, pltpu.roll(halo, d, axis=0), 0.0)
    xc = jnp.concatenate([xc[:HALO_ROWS] + head, xc[HALO_ROWS:]], axis=0)

    bd = xc.shape[1] // n_blocks
    xcb = xc.astype(BF16)
    gates = [_dot(xcb[:, g * bd:(g + 1) * bd], w_ref[g]) for g in range(n_blocks)]
    r = jax.nn.sigmoid(jnp.concatenate([gt[:, :bd] for gt in gates], axis=1) + ba_ref[...])
    i = jax.nn.sigmoid(jnp.concatenate([gt[:, bd:] for gt in gates], axis=1) + bx_ref[...])
    log_a = (-RG_C) * r * jax.nn.softplus(-lam_ref[...])
    a = jnp.exp(log_a)
    z = -jnp.tanh(log_a) * (a * a + 1.0)
    u = jnp.where(z > 0.0, z * lax.rsqrt(z), 0.0) * (i * xc)

    d = 1
    while d < ts:
        u = a * _shift_rows(u, d, 0.0) + u
        a = a * _shift_rows(a, d, 1.0)
        d *= 2
    hs = a * h_scr[...] + u
    h_scr[...] = hs[ts - 1:ts, :]
    o_ref[...] = (hs * jax.nn.gelu(gr_ref[...].astype(F32))).astype(o_ref.dtype)


def _rglru(proj3, conv_w, conv_b, w_cat, ba, bx, lam, *, ts=256):
    bsz, s, _ = proj3.shape
    c = conv_w.shape[1]
    n_blocks, bd, _ = w_cat.shape
    hb = ts // HALO_ROWS
    vec = pl.BlockSpec((1, c), lambda b, i: (0, 0))
    return pl.pallas_call(
        functools.partial(_rglru_kernel, ts=ts, n_blocks=n_blocks),
        grid=(bsz, s // ts),
        in_specs=[
            pl.BlockSpec((None, ts, c), lambda b, i: (b, i, 0)),
            pl.BlockSpec((None, HALO_ROWS, c), lambda b, i: (b, jnp.maximum(i * hb - 1, 0), 0)),
            pl.BlockSpec((None, ts, c), lambda b, i: (b, i, 1)),
            pl.BlockSpec((C_CONV, c), lambda b, i: (0, 0)),
            vec,
            pl.BlockSpec((n_blocks, bd, 2 * bd), lambda b, i: (0, 0, 0)),
            vec, vec, vec,
        ],
        out_specs=pl.BlockSpec((None, ts, c), lambda b, i: (b, i, 0)),
        out_shape=jax.ShapeDtypeStruct((bsz, s, c), BF16),
        scratch_shapes=[pltpu.VMEM((1, c), F32)],
        compiler_params=_cparams("parallel", "arbitrary"),
        name="rglru_scan",
    )(proj3, proj3, proj3, conv_w, conv_b.reshape(1, c), w_cat, ba.reshape(1, c), bx.reshape(1, c),
      lam.reshape(1, c))


def _mlstm_kernel(q_ref, k_ref, v_ref, og_ref, gt_ref, o_ref, c_scr, n_scr, m_scr, *, chunk, nh, dh):
    ci = pl.program_id(1)

    @pl.when(ci == 0)
    def _():
        c_scr[...] = jnp.zeros_like(c_scr)
        n_scr[...] = jnp.zeros_like(n_scr)
        m_scr[...] = jnp.zeros_like(m_scr)

    gates = gt_ref[...]
    bcum = jax.nn.log_sigmoid(gates)
    d = 1
    while d < chunk:
        bcum = bcum + _shift_rows(bcum, d, 0.0)
        d *= 2
    lane = lax.broadcasted_iota(jnp.int32, gates.shape, 1)
    rows_t = jnp.where(lane < nh, gates, bcum).T
    r_idx = lax.broadcasted_iota(jnp.int32, (chunk, chunk), 0)
    c_idx = lax.broadcasted_iota(jnp.int32, (chunk, chunk), 1)
    causal = c_idx <= r_idx
    k_scale = dh ** -0.5

    head_cols = [slice(h * dh, (h + 1) * dh) for h in range(nh)]
    qs = [q_ref[:, cols] for cols in head_cols]
    kss = [k_ref[:, cols] * jnp.asarray(k_scale, BF16) for cols in head_cols]
    vs = [v_ref[:, cols] for cols in head_cols]
    c_prevs = [c_scr[h] for h in range(nh)]
    qk = [_dot_nt(qs[h], kss[h]) for h in range(nh)]
    qc = [_dot(qs[h], c_prevs[h].astype(BF16)) for h in range(nh)]

    stage = []
    for h in range(nh):
        i_col = gates[:, h:h + 1]
        b_col = bcum[:, nh + h:nh + h + 1]
        i_row = rows_t[h:h + 1, :]
        b_row = rows_t[nh + h:nh + h + 1, :]
        m_prev = m_scr[h]

        dmat = jnp.where(causal, b_col - b_row + i_row, NEG_INF)
        inter = b_col + m_prev
        m_t = jnp.maximum(inter, jnp.max(dmat, axis=-1, keepdims=True))
        w_inter = jnp.exp(inter - m_t)
        s_qk = qk[h] * jnp.exp(dmat - m_t)
        n_prev = n_scr[h]
        q_n = jnp.sum(qs[h].astype(F32) * n_prev, axis=-1, keepdims=True)
        den = w_inter * q_n + jnp.sum(s_qk, axis=-1, keepdims=True)

        b_last = b_col[chunk - 1:chunk, :]
        w_s = b_last - b_col + i_col
        m_new = jnp.maximum(b_last + m_prev, jnp.max(w_s, axis=0, keepdims=True))
        decay = jnp.exp(b_last + m_prev - m_new)
        kw = jnp.exp(w_s - m_new) * kss[h].astype(F32)
        n_scr[h] = decay * n_prev + jnp.sum(kw, axis=0, keepdims=True)
        m_scr[h] = m_new
        stage.append((s_qk.astype(BF16), kw.astype(BF16), w_inter, den, m_t, decay))

    sv = [_dot(stage[h][0], vs[h]) for h in range(nh)]
    kv = [_dot_tn(stage[h][1], vs[h]) for h in range(nh)]
    for h in range(nh):
        _, _, w_inter, den, m_t, decay = stage[h]
        hid = (w_inter * qc[h] + sv[h]) / jnp.maximum(jnp.abs(den), jnp.exp(-m_t))
        c_scr[h] = decay * c_prevs[h] + kv[h]
        cols = head_cols[h]
        o_ref[:, cols] = (hid * jax.nn.sigmoid(og_ref[:, cols].astype(F32))).astype(o_ref.dtype)


def _mlstm(proj3, gates3, *, col0, nh, chunk=MLSTM_CHUNK):
    bsz, s, _ = proj3.shape
    dh = 256
    w = nh * dh
    cb = col0 // w
    blk = lambda k: pl.BlockSpec((None, chunk, w), lambda b, i: (b, i, cb + k))
    return pl.pallas_call(
        functools.partial(_mlstm_kernel, chunk=chunk, nh=nh, dh=dh),
        grid=(bsz, s // chunk),
        in_specs=[blk(0), blk(1), blk(2), blk(3),
                  pl.BlockSpec((None, chunk, GATE_LANES), lambda b, i: (b, i, 0))],
        out_specs=pl.BlockSpec((None, chunk, w), lambda b, i: (b, i, 0)),
        out_shape=jax.ShapeDtypeStruct((bsz, s, w), BF16),
        scratch_shapes=[pltpu.VMEM((nh, dh, dh), F32), pltpu.VMEM((nh, 1, dh), F32),
                        pltpu.VMEM((nh, 1, 1), F32)],
        compiler_params=_cparams("parallel", "arbitrary"),
        name="mlstm_chunkwise",
    )(proj3, proj3, proj3, proj3, gates3)


def kernel(x, c, mod_w, mod_b, norm_mix_g, norm_ffn_g, ffn_w1, ffn_w3, ffn_w2, ab_w_in, ab_conv_w, ab_w_out, cd_w_in, cd_gate_b, rg_conv_w, rg_conv_b, rg_wa, rg_ba, rg_wx, rg_bx, rg_lambda, cd_w_out, final_norm_g):
    bsz, seq, d = x.shape
    depth = mod_w.shape[0]
    t = bsz * seq
    half = d // 2
    a_heads = half // A_HEAD_DIM
    slopes = 2.0 ** (-8.0 * jnp.arange(1, a_heads + 1, dtype=F32) / a_heads)

    mod = _modulation(c, mod_w, mod_b)
    ffn_w1_q, ffn_s1 = _fp8_weights(ffn_w1)
    ffn_w3_q, ffn_s3 = _fp8_weights(ffn_w3)
    ffn_w2_b = ffn_w2.astype(BF16)
    ab_w_in_b, ab_w_out_b = ab_w_in.astype(BF16), ab_w_out.astype(BF16)
    cd_w_in_b, cd_w_out_b = cd_w_in.astype(BF16), cd_w_out.astype(BF16)
    x2 = x.reshape(t, d)
    for layer in range(depth):
        mod_l = mod[layer]
        if layer % 2 == 0:
            e = layer // 2
            n_main = 6 * half
            proj = _in_proj(x2, norm_mix_g[layer], mod_l, ab_w_in_b, e, n_main, seq, shift_row=0, scale_row=1)
            proj3 = proj.reshape(bsz, seq, n_main)
            y_a = _moba(proj3, slopes, a_heads).reshape(t, half)
            y_b, conv = proj, dict(conv_w=ab_conv_w[e], conv_col0=3 * half)
            w_out, w_idx = ab_w_out_b, e
        else:
            o = layer // 2
            n_main = 2 * half + 4 * half
            n_g = cd_w_in.shape[2] - n_main
            w_gate = jnp.zeros((d, GATE_LANES), BF16).at[:, :n_g].set(cd_w_in[o, :, n_main:].astype(BF16))
            b_gate = jnp.zeros((1, GATE_LANES), F32).at[0, :n_g].set(cd_gate_b[o])
            proj, gates = _in_proj(x2, norm_mix_g[layer], mod_l, cd_w_in_b, o, n_main, seq,
                                   shift_row=0, scale_row=1, w_gate=w_gate, b_gate=b_gate)
            proj3 = proj.reshape(bsz, seq, n_main)
            w_cat = jnp.concatenate([rg_wa[o], rg_wx[o]], axis=-1).astype(BF16)
            y_a = _rglru(proj3, rg_conv_w[o], rg_conv_b[o], w_cat, rg_ba[o], rg_bx[o],
                         rg_lambda[o]).reshape(t, half)
            y_b = _mlstm(proj3, gates.reshape(bsz, seq, GATE_LANES), col0=2 * half,
                         nh=D_HEADS).reshape(t, half)
            w_out, w_idx, conv = cd_w_out_b, o, {}
        x2 = _out_proj(y_a, y_b, w_out, w_idx, x2, mod_l, seq, gate_row=2, **conv)
        final_g = final_norm_g if layer == depth - 1 else None
        x2 = _ffn(x2, norm_ffn_g[layer], mod_l, ffn_w1_q, ffn_w3_q, ffn_s1, ffn_s3, ffn_w2_b,
                  layer, seq, final_g)
    return x2.reshape(bsz, seq, d)
```
